```python
import jax, jax.numpy as jnp
from jax import lax
import numpy as np

D_MODEL = 1024
BATCH = 8
SEQ = 2048
DEPTH = 1

GRID_W = 64
CTX_LEN = 256
CHUNK = 64
HG_HEADS = 4
HG_DK = 128
HG_DV = 128
HG_WIDTH = HG_HEADS * HG_DK
ML_HEADS = 4
ML_DH = 128
ML_WIDTH = ML_HEADS * ML_DH
D_MIX = HG_WIDTH + ML_WIDTH
CONV_K = 3
PEER_HEADS = 8
PEER_NKEYS = 128
PEER_EXPERTS = PEER_NKEYS * PEER_NKEYS
PEER_DQ = 256
PEER_TOPK = 16
PEER_BLOCK = 128
DEEPNORM_ALPHA = (2.0 * DEPTH) ** 0.25
DEEPNORM_BETA = (8.0 * DEPTH) ** -0.25
LN_EPS = 1e-6
IN_SPLITS = (HG_WIDTH, HG_WIDTH, HG_WIDTH, HG_WIDTH, HG_WIDTH,
             ML_WIDTH, ML_WIDTH, ML_WIDTH, ML_WIDTH, 4 * ML_HEADS)
D_IN = 5 * HG_WIDTH + 4 * ML_WIDTH + 4 * ML_HEADS

kernel_name = "hybrid_hgrn2_mlstm_peer_diffusion_layer"


def _layer_norm(x):
    xf = x.astype(jnp.float32)
    mu = jnp.mean(xf, axis=-1, keepdims=True)
    var = jnp.mean(jnp.square(xf - mu), axis=-1, keepdims=True)
    return ((xf - mu) * lax.rsqrt(var + LN_EPS)).astype(x.dtype)


def _affine_ln(x, g, b):
    return _layer_norm(x) * g + b


def _modulate(x, shift, scale):
    return _layer_norm(x) * (1 + scale) + shift


def _to_heads(t, n_heads):
    b, l, w = t.shape
    return t.reshape(b, l, n_heads, w // n_heads).transpose(0, 2, 1, 3)


def _from_heads(t):
    b, h, l, d = t.shape
    return t.transpose(0, 2, 1, 3).reshape(b, l, h * d)


def _head_rms_norm(o, g):
    of = o.astype(jnp.float32)
    y = of * lax.rsqrt(jnp.mean(of * of, axis=-1, keepdims=True) + LN_EPS)
    return y * g.reshape(o.shape[1], 1, o.shape[-1])


def _head_layer_norm(o, g):
    return _layer_norm(o.astype(jnp.float32)) * g.reshape(o.shape[1], 1, o.shape[-1])


def _split_projection(p):
    offsets = np.cumsum(IN_SPLITS)[:-1].tolist()
    return jnp.split(p, offsets, axis=-1)


def _conv_grid(t, w, b):
    bsz, l, ch = t.shape
    rows = l // GRID_W
    y = lax.conv_general_dilated(t.reshape(bsz, rows, GRID_W, ch), w[:, :, None, :], (1, 1), 'SAME',
                                 dimension_numbers=('NHWC', 'HWIO', 'NHWC'), feature_group_count=ch)
    return y.reshape(bsz, l, ch) + b


def _conv_seq(t, w_row, b):
    ch = t.shape[-1]
    y = lax.conv_general_dilated(t, w_row[:, None, :], (1,), 'SAME',
                                 dimension_numbers=('NWC', 'WIO', 'NWC'), feature_group_count=ch)
    return y + b


def _join(a_ctx, a_lat, reverse):
    if reverse:
        a_ctx, a_lat = jnp.flip(a_ctx, axis=2), jnp.flip(a_lat, axis=2)
    return jnp.concatenate([a_ctx, a_lat], axis=2)


def _unjoin(o, n_ctx, reverse):
    o_ctx, o_lat = o[:, :, :n_ctx], o[:, :, n_ctx:]
    if reverse:
        o_ctx, o_lat = jnp.flip(o_ctx, axis=2), jnp.flip(o_lat, axis=2)
    return o_ctx, o_lat


def _bidirectional(scan_fn, shared_ctx, shared_lat, dir_ctx, dir_lat):
    n_ctx = shared_ctx[0].shape[2]
    outs = []
    for d, rev in enumerate((False, True)):
        args = [_join(a_c, a_l, rev) for a_c, a_l in zip(shared_ctx + dir_ctx[d], shared_lat + dir_lat[d])]
        outs.append(_unjoin(scan_fn(*args), n_ctx, rev))
    return outs[0][0] + outs[1][0], outs[0][1] + outs[1][1]


def _gla_chunkwise(q, v, k, log_f):
    bsz, h, t, dk = q.shape
    dv = v.shape[-1]
    n = t // CHUNK
    q, k, log_f = (a.reshape(bsz, h, n, CHUNK, dk) for a in (q, k, log_f))
    v = v.reshape(bsz, h, n, CHUNK, dv)
    b = jnp.cumsum(log_f.astype(jnp.float32), axis=3)
    b_ref = b[:, :, :, CHUNK // 2:CHUNK // 2 + 1]
    lower = jnp.tril(jnp.ones((CHUNK, CHUNK), dtype=bool))
    scores = jnp.einsum('bhnck,bhnsk->bhncs', q * jnp.exp(b - b_ref), k * jnp.exp(b_ref - b))
    o = jnp.einsum('bhncs,bhnsv->bhncv', jnp.where(lower, scores, 0.0), v)
    b_last = b[:, :, :, -1:]
    upd = jnp.einsum('bhnsk,bhnsv->bhnkv', k * jnp.exp(b_last - b), v)

    def step(s, xs):
        decay, u = xs
        return decay[..., None] * s + u, s

    s0 = jnp.zeros((bsz, h, dk, dv), upd.dtype)
    _, s_prev = lax.scan(step, s0, (jnp.moveaxis(jnp.exp(b_last[:, :, :, 0]), 2, 0), jnp.moveaxis(upd, 2, 0)))
    s_prev = jnp.moveaxis(s_prev, 0, 2)
    o = o + jnp.einsum('bhnck,bhnkv->bhncv', q * jnp.exp(b), s_prev)
    return o.reshape(bsz, h, t, dv)


def _mlstm_chunkwise(q, k, v, log_i, log_f):
    bsz, h, t, d = q.shape
    n = t // CHUNK
    q, k, v = (a.reshape(bsz, h, n, CHUNK, d) for a in (q, k, v))
    log_i = log_i.astype(jnp.float32).reshape(bsz, h, n, CHUNK)
    log_f = log_f.astype(jnp.float32).reshape(bsz, h, n, CHUNK)
    a = jnp.cumsum(log_f, axis=-1)
    a_last = a[..., -1]
    lower = jnp.tril(jnp.ones((CHUNK, CHUNK), dtype=bool))
    log_d = jnp.where(lower, a[..., :, None] - a[..., None, :] + log_i[..., None, :], -jnp.inf)
    g = a_last[..., None] - a + log_i
    m_loc = jnp.max(g, axis=-1)
    w = jnp.exp(g - m_loc[..., None])
    upd_c = jnp.einsum('bhnc,bhnck,bhncv->bhnkv', w, k, v)
    upd_n = jnp.einsum('bhnc,bhnck->bhnk', w, k)

    def step(carry, xs):
        c_st, n_st, m = carry
        al, ml, uc, un = xs
        m_new = jnp.maximum(al + m, ml)
        s_old = jnp.exp(al + m - m_new)
        s_new = jnp.exp(ml - m_new)
        c_new = s_old[..., None, None] * c_st + s_new[..., None, None] * uc
        n_new = s_old[..., None] * n_st + s_new[..., None] * un
        return (c_new, n_new, m_new), (c_st, n_st, m)

    init = (jnp.zeros((bsz, h, d, d), upd_c.dtype), jnp.zeros((bsz, h, d), upd_n.dtype),
            jnp.zeros((bsz, h), jnp.float32))
    xs = tuple(jnp.moveaxis(a_, 2, 0) for a_ in (a_last, m_loc, upd_c, upd_n))
    _, (c_prev, n_prev, m_prev) = lax.scan(step, init, xs)
    c_prev, n_prev, m_prev = (jnp.moveaxis(a_, 0, 2) for a_ in (c_prev, n_prev, m_prev))
    m_inter = a + m_prev[..., None]
    m_t = jnp.maximum(jnp.max(log_d, axis=-1), m_inter)
    w_intra = jnp.einsum('bhnck,bhnsk->bhncs', q, k) * jnp.exp(log_d - m_t[..., None])
    w_inter = jnp.exp(m_inter - m_t)
    num = (jnp.einsum('bhncs,bhnsv->bhncv', w_intra, v)
           + w_inter[..., None] * jnp.einsum('bhnck,bhnkv->bhncv', q, c_prev))
    den = jnp.sum(w_intra, axis=-1) + w_inter * jnp.einsum('bhnck,bhnk->bhnc', q, n_prev)
    hout = num / jnp.maximum(jnp.abs(den), jnp.exp(-m_t))[..., None]
    return hout.reshape(bsz, h, t, d)


def _hgrn_gates(z, lb):
    zf = z.astype(jnp.float32)
    log_f = jnp.log(lb + (1.0 - lb) * jax.nn.sigmoid(zf))
    key = (1.0 - lb) * jax.nn.sigmoid(-zf)
    return key, log_f


def _mlstm_gates(gt, gate_b):
    g = jnp.moveaxis((gt + gate_b).astype(jnp.float32), -1, 1)
    hh = ML_HEADS
    return ((g[:, 0:hh], jax.nn.log_sigmoid(g[:, 2 * hh:3 * hh])),
            (g[:, hh:2 * hh], jax.nn.log_sigmoid(g[:, 3 * hh:4 * hh])))


def _readout(o_hg, h_ml, hg_gate, ml_gate, hg_norm_g, ml_norm_g):
    a = _from_heads(_head_rms_norm(o_hg, hg_norm_g)) * jax.nn.silu(hg_gate)
    b = _from_heads(_head_layer_norm(h_ml, ml_norm_g)) * jax.nn.sigmoid(ml_gate)
    return jnp.concatenate([a, b], axis=-1)


def _token_mixers(p_ctx, p_lat, lb, hg_norm_g, conv_w, conv_b, gate_b, ml_norm_g):
    hq_c, hi_c, hg_c, hf0_c, hf1_c, mq_c, mk_c, mv_c, mo_c, mgt_c = _split_projection(p_ctx)
    hq_l, hi_l, hg_l, hf0_l, hf1_l, mq_l, mk_l, mv_l, mo_l, mgt_l = _split_projection(p_lat)

    def hg_dirs(f0, f1):
        return tuple(tuple(_to_heads(a_, HG_HEADS) for a_ in _hgrn_gates(f, lb[d]))
                     for d, f in enumerate((f0, f1)))

    o_hg_c, o_hg_l = _bidirectional(
        _gla_chunkwise,
        (_to_heads(hq_c, HG_HEADS), _to_heads(hi_c, HG_HEADS)),
        (_to_heads(hq_l, HG_HEADS), _to_heads(hi_l, HG_HEADS)),
        hg_dirs(hf0_c, hf1_c), hg_dirs(hf0_l, hf1_l))

    qk_c = jax.nn.silu(_conv_seq(jnp.concatenate([mq_c, mk_c], axis=-1), conv_w[CONV_K // 2], conv_b))
    qk_l = jax.nn.silu(_conv_grid(jnp.concatenate([mq_l, mk_l], axis=-1), conv_w, conv_b))
    kscale = ML_DH ** -0.5

    def ml_shared(qk, v):
        q_, k_ = jnp.split(qk, 2, axis=-1)
        return (_to_heads(q_, ML_HEADS), _to_heads(k_ * kscale, ML_HEADS), _to_heads(v, ML_HEADS))

    h_ml_c, h_ml_l = _bidirectional(
        _mlstm_chunkwise, ml_shared(qk_c, mv_c), ml_shared(qk_l, mv_l),
        _mlstm_gates(mgt_c, gate_b), _mlstm_gates(mgt_l, gate_b))

    mix_c = _readout(o_hg_c, h_ml_c, hg_c, mo_c, hg_norm_g, ml_norm_g)
    mix_l = _readout(o_hg_l, h_ml_l, hg_l, mo_l, hg_norm_g, ml_norm_g)
    return mix_c, mix_l


def _peer(h, wq, keys, u_tab, v_tab):
    bsz, l, dm = h.shape
    q = (h @ wq).reshape(bsz, l, PEER_HEADS, 2, PEER_DQ // 2)
    s = jnp.einsum('blhpd,hpkd->blhpk', q, keys).astype(jnp.float32)
    top_s, top_i = lax.top_k(s, PEER_TOPK)
    cand = (top_s[..., 0, :, None] + top_s[..., 1, None, :]).reshape(bsz, l, PEER_HEADS, PEER_TOPK * PEER_TOPK)
    best_s, best_c = lax.top_k(cand, PEER_TOPK)
    idx = (jnp.take_along_axis(top_i[..., 0, :], best_c // PEER_TOPK, axis=-1) * PEER_NKEYS
           + jnp.take_along_axis(top_i[..., 1, :], best_c % PEER_TOPK, axis=-1))
    gate = jax.nn.softmax(best_s, axis=-1)
    n_blk = (bsz * l) // PEER_BLOCK
    hb = h.reshape(n_blk, PEER_BLOCK, dm)
    ib = idx.reshape(n_blk, PEER_BLOCK, PEER_HEADS * PEER_TOPK)
    gb = gate.reshape(n_blk, PEER_BLOCK, PEER_HEADS * PEER_TOPK).astype(h.dtype)

    def block(args):
        hx, ix, gx = args
        act = jax.nn.gelu(jnp.einsum('pd,ped->pe', hx, u_tab[ix]), approximate=False)
        return jnp.einsum('pe,ped->pd', gx * act, v_tab[ix])

    return lax.map(block, (hb, ib, gb)).reshape(bsz, l, dm)


def setup_inputs(seed: int = 0) -> dict:
    key = jax.random.key(seed)
    ks = jax.random.split(key, 26)
    f32 = jnp.float32

    def nrm(k, shape, scale):
        return jax.random.normal(k, shape, f32) * scale

    forget_bias = jnp.tile(jnp.linspace(3.0, 6.0, ML_HEADS, dtype=f32), (DEPTH, 2))
    ml_gate_b = jnp.concatenate([nrm(ks[11], (DEPTH, 2 * ML_HEADS), 0.1),
                                 forget_bias + nrm(ks[12], (DEPTH, 2 * ML_HEADS), 0.1)], axis=-1)
    return {
        "x": nrm(ks[0], (BATCH, SEQ, D_MODEL), 1.0),
        "c": nrm(ks[1], (BATCH, D_MODEL), 1.0),
        "ctx": nrm(ks[2], (BATCH, CTX_LEN, D_MODEL), 1.0),
        "c_ctx": nrm(ks[3], (D_MODEL,), 1.0),
        "w_mod": nrm(ks[4], (DEPTH, D_MODEL, 6 * D_MODEL), D_MODEL ** -0.5),
        "b_mod": nrm(ks[5], (DEPTH, 6 * D_MODEL), 0.02),
        "w_in": nrm(ks[6], (DEPTH, D_MODEL, D_IN), D_MODEL ** -0.5),
        "hg_lb_logits": nrm(ks[7], (2, DEPTH + 1, HG_WIDTH), 0.1),
        "hg_norm_g": 1.0 + nrm(ks[8], (DEPTH, HG_WIDTH), 0.02),
        "ml_conv_w": nrm(ks[9], (DEPTH, CONV_K, CONV_K, 2 * ML_WIDTH), 1.0 / CONV_K),
        "ml_conv_b": nrm(ks[10], (DEPTH, 2 * ML_WIDTH), 0.02),
        "ml_gate_b": ml_gate_b,
        "ml_norm_g": 1.0 + nrm(ks[13], (DEPTH, ML_WIDTH), 0.02),
        "w_out": nrm(ks[14], (DEPTH, D_MIX, D_MODEL), DEEPNORM_BETA * D_MIX ** -0.5),
        "ln1_g": 1.0 + nrm(ks[15], (DEPTH, D_MODEL), 0.02),
        "ln1_b": nrm(ks[16], (DEPTH, D_MODEL), 0.02),
        "peer_wq": nrm(ks[17], (DEPTH, D_MODEL, PEER_HEADS * PEER_DQ), D_MODEL ** -0.5),
        "peer_keys": nrm(ks[18], (DEPTH, PEER_HEADS, 2, PEER_NKEYS, PEER_DQ // 2), (PEER_DQ // 2) ** -0.5),
        "peer_u": nrm(ks[19], (DEPTH, PEER_EXPERTS, D_MODEL), D_MODEL ** -0.5),
        "peer_v": nrm(ks[20], (DEPTH, PEER_EXPERTS, D_MODEL), DEEPNORM_BETA),
        "ln2_g": 1.0 + nrm(ks[21], (DEPTH, D_MODEL), 0.02),
        "ln2_b": nrm(ks[22], (DEPTH, D_MODEL), 0.02),
    }


def reference(x, c, ctx, c_ctx, w_mod, b_mod, w_in, hg_lb_logits, hg_norm_g, ml_conv_w, ml_conv_b,
              ml_gate_b, ml_norm_g, w_out, ln1_g, ln1_b, peer_wq, peer_keys, peer_u, peer_v, ln2_g, ln2_b):
    lower_bounds = jnp.cumsum(jax.nn.softmax(hg_lb_logits.astype(jnp.float32), axis=1), axis=1)
    for l in range(DEPTH):
        mod_l = jax.nn.silu(c) @ w_mod[l] + b_mod[l]
        mod_c = jax.nn.silu(c_ctx) @ w_mod[l] + b_mod[l]
        sh1_l, sc1_l, g1_l, sh2_l, sc2_l, g2_l = jnp.split(mod_l[:, None, :], 6, axis=-1)
        sh1_c, sc1_c, g1_c, sh2_c, sc2_c, g2_c = jnp.split(mod_c[None, None, :], 6, axis=-1)

        p_c = _modulate(ctx, sh1_c, sc1_c) @ w_in[l]
        p_l = _modulate(x, sh1_l, sc1_l) @ w_in[l]
        mix_c, mix_l = _token_mixers(p_c, p_l, lower_bounds[:, l], hg_norm_g[l], ml_conv_w[l],
                                     ml_conv_b[l], ml_gate_b[l], ml_norm_g[l])
        x_new = _affine_ln(DEEPNORM_ALPHA * x + g1_l * (mix_l @ w_out[l]), ln1_g[l], ln1_b[l])
        if l < DEPTH - 1:
            ctx = _affine_ln(DEEPNORM_ALPHA * ctx + g1_c * (mix_c @ w_out[l]), ln1_g[l], ln1_b[l])
            ctx = _affine_ln(DEEPNORM_ALPHA * ctx
                             + g2_c * _peer(_modulate(ctx, sh2_c, sc2_c), peer_wq[l], peer_keys[l], peer_u[l], peer_v[l]),
                             ln2_g[l], ln2_b[l])
        x = x_new

        y = _peer(_modulate(x, sh2_l, sc2_l), peer_wq[l], peer_keys[l], peer_u[l], peer_v[l])
        x = _affine_ln(DEEPNORM_ALPHA * x + g2_l * y, ln2_g[l], ln2_b[l])
    return x
```

```python
import functools
import math

import jax
import jax.numpy as jnp
from jax import lax
from jax.experimental import pallas as pl
from jax.experimental.pallas import tpu as pltpu

F32 = jnp.float32
BF16 = jnp.bfloat16

D_MODEL = 1024
CHUNK = 64
GRID_W = 64
N_HEADS = 4
D_HEAD = 128
MIX_W = N_HEADS * D_HEAD
N_GATES = 4 * N_HEADS
D_MAIN = 9 * MIX_W
PEER_HEADS = 8
PEER_NKEYS = 128
PEER_TOPK = 16
PEER_DQ = 256
LN_EPS = 1e-6
DEEPNORM_ALPHA = 2.0 ** 0.25
LANES = 128
VMEM_LIMIT = 56 * 1024 * 1024

COL_HQ, COL_HI, COL_HG, COL_HF0, COL_HF1, COL_MQ, COL_MK, COL_MV, COL_MO = range(9)

ROW_TILE = 256
SEL_TILE = 256
PEER_TT = 256
PEER_ET = 1024

NT_DIMS = (((1,), (1,)), ((), ()))
TN_DIMS = (((0,), (0,)), ((), ()))


def _cparams(sem):
    return pltpu.CompilerParams(dimension_semantics=sem, vmem_limit_bytes=VMEM_LIMIT)


def _split(a, n):
    parts = []
    r = a
    for k in range(n):
        p = r.astype(BF16)
        parts.append(p)
        if k + 1 < n:
            r = r - p.astype(F32)
    return parts


def _mm(a, b, dims=None):
    if dims is None:
        return jnp.dot(a, b, preferred_element_type=F32)
    return lax.dot_general(a, b, dims, preferred_element_type=F32)


def _mm_bf16(a, b, dims=None):
    return _mm(a.astype(BF16), b.astype(BF16), dims)


def _mm_x3(a, b, dims=None):
    a_hi, a_lo = _split(a, 2)
    b_hi, b_lo = _split(b, 2)
    return _mm(a_hi, b_hi, dims) + (_mm(a_hi, b_lo, dims) + _mm(a_lo, b_hi, dims))


def _mm_x3_pre(a, b_hi, b_lo, dims=None):
    a_hi, a_lo = _split(a, 2)
    return _mm(a_hi, b_hi, dims) + (_mm(a_hi, b_lo, dims) + _mm(a_lo, b_hi, dims))


def _mm_exact_left(m_bf16, x):
    x0, x1, x2 = _split(x, 3)
    return _mm(m_bf16, x0) + (_mm(m_bf16, x1) + _mm(m_bf16, x2))


def _mm_exact_right(x, m_bf16):
    x0, x1, x2 = _split(x, 3)
    return _mm(x0, m_bf16) + (_mm(x1, m_bf16) + _mm(x2, m_bf16))


def _sigmoid(x):
    return 1.0 / (1.0 + jnp.exp(-x))


def _silu(x):
    return x * _sigmoid(x)


def _log_sigmoid(x):
    return jnp.minimum(x, 0.0) - jnp.log(1.0 + jnp.exp(-jnp.abs(x)))


def _ln(x):
    mu = jnp.mean(x, axis=-1, keepdims=True)
    xc = x - mu
    var = jnp.mean(xc * xc, axis=-1, keepdims=True)
    return xc * lax.rsqrt(var + LN_EPS)


def _gelu(x):
    return 0.5 * x * (1.0 + lax.erf(x * (2.0 ** -0.5)))


def _mod_kernel(c_ref, w_ref, b_ref, o_ref):
    s = _silu(c_ref[...])
    o_ref[...] = _mm_x3(s, w_ref[...]) + b_ref[...]


def _modulation(c_all, w_mod, b_mod):
    n = c_all.shape[0]
    d_out = w_mod.shape[1]
    blk = 1024
    return pl.pallas_call(
        _mod_kernel,
        grid=(d_out // blk,),
        in_specs=[pl.BlockSpec((n, D_MODEL), lambda j: (0, 0)),
                  pl.BlockSpec((D_MODEL, blk), lambda j: (0, j)),
                  pl.BlockSpec((1, blk), lambda j: (0, j))],
        out_specs=pl.BlockSpec((n, blk), lambda j: (0, j)),
        out_shape=jax.ShapeDtypeStruct((n, d_out), F32),
        compiler_params=_cparams(("arbitrary",)),
        name="mod",
    )(c_all, w_mod, b_mod.reshape(1, d_out))


def _inproj_kernel(x_ref, mod_ref, w_ref, wg_hi_ref, wg_lo_ref, wgt_hi_ref, wgt_lo_ref, gb_ref, gbt_ref,
                   p_ref, g_ref, gt_ref):
    x = x_ref[0]
    shift = mod_ref[0, 0, :, :D_MODEL]
    scale = mod_ref[0, 0, :, D_MODEL:]
    h = _ln(x) * (1.0 + scale) + shift
    p_ref[0] = _mm(h.astype(BF16), w_ref[...])
    h_hi, h_lo = _split(h, 2)
    g = _mm(h_hi, wg_hi_ref[...]) + (_mm(h_hi, wg_lo_ref[...]) + _mm(h_lo, wg_hi_ref[...]))
    g_ref[0] = g + gb_ref[...]
    gt = (_mm(wgt_hi_ref[...], h_hi, NT_DIMS)
          + (_mm(wgt_lo_ref[...], h_hi, NT_DIMS) + _mm(wgt_hi_ref[...], h_lo, NT_DIMS)))
    gt_ref[0] = gt + gbt_ref[...]


def _input_projection(x_all, mod1, w_main, wg, gate_b):
    bsz, t_all, _ = x_all.shape
    n_tiles = t_all // ROW_TILE
    wg_pad = jnp.zeros((D_MODEL, LANES), F32).at[:, :N_GATES].set(wg)
    wg_hi = wg_pad.astype(BF16)
    wg_lo = (wg_pad - wg_hi.astype(F32)).astype(BF16)
    wgt = wg.T
    wgt_hi = wgt.astype(BF16)
    wgt_lo = (wgt - wgt_hi.astype(F32)).astype(BF16)
    gb = jnp.zeros((1, LANES), F32).at[0, :N_GATES].set(gate_b)
    gbt = gate_b.reshape(N_GATES, 1)
    const2 = lambda b, t: (0, 0)
    return pl.pallas_call(
        _inproj_kernel,
        grid=(bsz, n_tiles),
        in_specs=[pl.BlockSpec((1, ROW_TILE, D_MODEL), lambda b, t: (b, t, 0)),
                  pl.BlockSpec((1, 1, 1, 2 * D_MODEL), lambda b, t: (b, t, 0, 0)),
                  pl.BlockSpec((D_MODEL, D_MAIN), const2),
                  pl.BlockSpec((D_MODEL, LANES), const2),
                  pl.BlockSpec((D_MODEL, LANES), const2),
                  pl.BlockSpec((N_GATES, D_MODEL), const2),
                  pl.BlockSpec((N_GATES, D_MODEL), const2),
                  pl.BlockSpec((1, LANES), const2),
                  pl.BlockSpec((N_GATES, 1), const2)],
        out_specs=[pl.BlockSpec((1, ROW_TILE, D_MAIN), lambda b, t: (b, t, 0)),
                   pl.BlockSpec((1, ROW_TILE, LANES), lambda b, t: (b, t, 0)),
                   pl.BlockSpec((1, N_GATES, ROW_TILE), lambda b, t: (b, 0, t))],
        out_shape=[jax.ShapeDtypeStruct((bsz, t_all, D_MAIN), F32),
                   jax.ShapeDtypeStruct((bsz, t_all, LANES), F32),
                   jax.ShapeDtypeStruct((bsz, N_GATES, t_all), F32)],
        compiler_params=_cparams(("arbitrary", "arbitrary")),
        name="inproj",
    )(x_all, mod1, w_main, wg_hi, wg_lo, wgt_hi, wgt_lo, gb, gbt)


CONV_PAD = 72
CONV_CB = 256


def _conv_kernel(n_ctx, n_lat, p_ref, w_ref, b_ref, o_ref, pad_c, pad_l):
    cb = pl.program_id(1)
    w = w_ref[...]
    bias = b_ref[...]
    scale = jnp.where(cb >= MIX_W // CONV_CB, D_HEAD ** -0.5, 1.0).astype(F32)

    def finish(y):
        return _silu(y + bias) * scale

    pad_c[...] = jnp.zeros_like(pad_c)
    pad_c[pl.ds(CONV_PAD, n_ctx), :] = p_ref[0, pl.ds(0, n_ctx), :]
    y = jnp.zeros((n_ctx, CONV_CB), F32)
    for dc in range(3):
        y = y + pad_c[pl.ds(CONV_PAD + dc - 1, n_ctx), :] * w[3 + dc:4 + dc, :]
    o_ref[0, pl.ds(0, n_ctx), :] = finish(y)

    pad_l[...] = jnp.zeros_like(pad_l)
    pad_l[pl.ds(CONV_PAD, n_lat), :] = p_ref[0, pl.ds(n_ctx, n_lat), :]
    col = lax.broadcasted_iota(jnp.int32, (n_lat, CONV_CB), 0) & (GRID_W - 1)
    y = jnp.zeros((n_lat, CONV_CB), F32)
    for dr in range(3):
        for dc in range(3):
            shift = (dr - 1) * GRID_W + (dc - 1)
            t = pad_l[pl.ds(CONV_PAD + shift, n_lat), :] * w[3 * dr + dc:3 * dr + dc + 1, :]
            if dc == 0:
                t = jnp.where(col >= 1, t, 0.0)
            elif dc == 2:
                t = jnp.where(col <= GRID_W - 2, t, 0.0)
            y = y + t
    o_ref[0, pl.ds(n_ctx, n_lat), :] = finish(y)


def _qk_conv(p, conv_w, conv_b, n_ctx):
    bsz, t_all, _ = p.shape
    n_lat = t_all - n_ctx
    n_cb = 2 * MIX_W // CONV_CB
    first = COL_MQ * MIX_W // CONV_CB
    w9 = conv_w.reshape(9, 2 * MIX_W)
    return pl.pallas_call(
        functools.partial(_conv_kernel, n_ctx, n_lat),
        grid=(bsz, n_cb),
        in_specs=[pl.BlockSpec((1, t_all, CONV_CB), lambda b, c: (b, 0, first + c)),
                  pl.BlockSpec((9, CONV_CB), lambda b, c: (0, c)),
                  pl.BlockSpec((1, CONV_CB), lambda b, c: (0, c))],
        out_specs=pl.BlockSpec((1, t_all, CONV_CB), lambda b, c: (b, 0, c)),
        out_shape=jax.ShapeDtypeStruct((bsz, t_all, 2 * MIX_W), F32),
        scratch_shapes=[pltpu.VMEM((n_ctx + 2 * CONV_PAD, CONV_CB), F32),
                        pltpu.VMEM((n_lat + 2 * CONV_PAD, CONV_CB), F32)],
        compiler_params=_cparams(("arbitrary", "arbitrary")),
        name="qkconv",
    )(p, w9, conv_b.reshape(1, 2 * MIX_W))


def _tri(rev):
    r = lax.broadcasted_iota(jnp.int32, (CHUNK, CHUNK), 0)
    c = lax.broadcasted_iota(jnp.int32, (CHUNK, CHUNK), 1)
    return (c >= r) if rev else (c <= r)


def _hgrn_dir(d, rev, want_out, q_ref, v_ref, z_ref, lb, st_ref, o_ref):
    v = v_ref[0]
    z = z_ref[0]
    sig = _sigmoid(z)
    nsig = _sigmoid(-z)
    log_f = jnp.log(lb + (1.0 - lb) * sig)
    kk = (1.0 - lb) * nsig
    mask = _tri(rev)
    tri = jnp.where(mask, 1.0, 0.0).astype(BF16)
    b = _mm_exact_left(tri, log_f)
    last = 0 if rev else CHUNK - 1
    mid = CHUNK // 2 - 1 if rev else CHUNK // 2
    b_last = b[last:last + 1, :]
    b_mid = b[mid:mid + 1, :]
    kdec = (kk * jnp.exp(b_last - b)).astype(BF16)
    e_last = jnp.exp(b_last)
    vb = v.astype(BF16)

    @pl.when(want_out)
    def _():
        q = q_ref[0]
        qd = (q * jnp.exp(b - b_mid)).astype(BF16)
        kd = (kk * jnp.exp(b_mid - b)).astype(BF16)
        qe = (q * jnp.exp(b)).astype(BF16)
        for h in range(N_HEADS):
            sl = slice(h * D_HEAD, (h + 1) * D_HEAD)
            sc = _mm(qd[:, sl], kd[:, sl], NT_DIMS)
            sc = jnp.where(mask, sc, 0.0).astype(BF16)
            o = _mm(sc, vb[:, sl]) + _mm(qe[:, sl], st_ref[d, h].astype(BF16), NT_DIMS)
            o_ref[0, :, sl] = o

    for h in range(N_HEADS):
        sl = slice(h * D_HEAD, (h + 1) * D_HEAD)
        st_ref[d, h] = st_ref[d, h] * e_last[:, sl] + _mm(vb[:, sl], kdec[:, sl], TN_DIMS)


def _mlstm_dir(d, rev, want_out, qk_ref, v_ref, gc_ref, gr_ref, c_ref, n_ref, m_ref, o_ref):
    mask = _tri(rev)
    tri = jnp.where(mask, 1.0, 0.0).astype(BF16)
    tri_t = jnp.where(_tri(not rev), 1.0, 0.0).astype(BF16)
    gc = gc_ref[0]
    gr = gr_ref[0, 0]
    a_cols = _mm_exact_left(tri, _log_sigmoid(gc))
    a_rows = _mm_exact_right(_log_sigmoid(gr), tri_t)
    last = 0 if rev else CHUNK - 1
    qk = qk_ref[0]
    v = v_ref[0]
    vb = v.astype(BF16)
    for h in range(N_HEADS):
        ic = d * N_HEADS + h
        fc = 2 * N_HEADS + d * N_HEADS + h
        sl = slice(h * D_HEAD, (h + 1) * D_HEAD)
        q_h = qk[:, h * D_HEAD:(h + 1) * D_HEAD]
        k_h = qk[:, MIX_W + h * D_HEAD:MIX_W + (h + 1) * D_HEAD]
        v_h = vb[:, sl]
        a_c = a_cols[:, fc:fc + 1]
        a_r = a_rows[fc:fc + 1, :]
        li_c = gc[:, ic:ic + 1]
        li_r = gr[ic:ic + 1, :]
        a_last = a_r[:, last:last + 1]
        m_prev = m_ref[d, h][:, 0:1]
        c_prev = c_ref[d, h]
        n_prev = n_ref[d, h]

        @pl.when(want_out)
        def _():
            m_inter = a_c + m_prev
            log_d = jnp.where(mask, a_c - a_r + li_r, -jnp.inf)
            m_t = jnp.maximum(jnp.max(log_d, axis=1, keepdims=True), m_inter)
            s_qk = _mm_bf16(q_h, k_h, NT_DIMS)
            w_intra = s_qk * jnp.exp(log_d - m_t)
            w_inter = jnp.exp(m_inter - m_t)
            num = _mm(w_intra.astype(BF16), v_h) + w_inter * _mm_bf16(q_h, c_prev)
            qn = jnp.sum(q_h * n_prev, axis=1, keepdims=True)
            den = jnp.sum(w_intra, axis=1, keepdims=True) + w_inter * qn
            o_ref[0, :, MIX_W + h * D_HEAD:MIX_W + (h + 1) * D_HEAD] = (
                num / jnp.maximum(jnp.abs(den), jnp.exp(-m_t)))

        g_c = a_last - a_c + li_c
        m_loc = jnp.max(g_c, axis=0, keepdims=True)
        wk = jnp.exp(g_c - m_loc) * k_h
        upd_c = _mm(wk.astype(BF16), v_h, TN_DIMS)
        upd_n = jnp.sum(wk, axis=0, keepdims=True)
        m_new = jnp.maximum(a_last + m_prev, m_loc)
        s_old = jnp.exp(a_last + m_prev - m_new)
        s_new = jnp.exp(m_loc - m_new)
        c_ref[d, h] = s_old * c_prev + s_new * upd_c
        n_ref[d, h] = s_old * n_prev + s_new * upd_n
        m_ref[d, h] = jnp.broadcast_to(m_new, (1, LANES))


def _mix_kernel(n_cc, lbl_ref,
                hq_f, hi_f, hf_f, mv_f, qk_f, gc_f, gr_f,
                hq_b, hi_b, hf_b, mv_b, qk_b, gc_b, gr_b,
                of_ref, ob_ref, st_ref, c_ref, n_ref, m_ref):
    s = pl.program_id(1)

    @pl.when(s == 0)
    def _():
        st_ref[...] = jnp.zeros_like(st_ref)
        c_ref[...] = jnp.zeros_like(c_ref)
        n_ref[...] = jnp.zeros_like(n_ref)
        m_ref[...] = jnp.zeros_like(m_ref)

    want_out = s >= n_cc
    for d, rev, (hq, hi, hf, mv, qk, gc, gr, o_ref) in (
            (0, False, (hq_f, hi_f, hf_f, mv_f, qk_f, gc_f, gr_f, of_ref)),
            (1, True, (hq_b, hi_b, hf_b, mv_b, qk_b, gc_b, gr_b, ob_ref))):
        logits = lbl_ref[d]
        mx = jnp.max(logits, axis=0, keepdims=True)
        ex = jnp.exp(logits - mx)
        lb = ex[0:1, :] / jnp.sum(ex, axis=0, keepdims=True)
        _hgrn_dir(d, rev, want_out, hq, hi, hf, lb, st_ref, o_ref)
        _mlstm_dir(d, rev, want_out, qk, mv, gc, gr, c_ref, n_ref, m_ref, o_ref)


def _token_mixers(p, qk, gates, gates_rows, lb_logits, n_ctx):
    bsz, t_all, _ = p.shape
    n_lat = t_all - n_ctx
    n_cc = n_ctx // CHUNK
    n_ch = t_all // CHUNK

    def f_idx(s):
        return s

    def b_idx(s):
        return jnp.where(s < n_cc, n_cc - 1 - s, n_ch - 1 + n_cc - s)

    def pspec(col, idx):
        return pl.BlockSpec((1, CHUNK, MIX_W), lambda b, s: (b, idx(s), col))

    def dir_specs(idx, col_f):
        return [pspec(COL_HQ, idx), pspec(COL_HI, idx), pspec(col_f, idx), pspec(COL_MV, idx),
                pl.BlockSpec((1, CHUNK, 2 * MIX_W), lambda b, s: (b, idx(s), 0)),
                pl.BlockSpec((1, CHUNK, LANES), lambda b, s: (b, idx(s), 0)),
                pl.BlockSpec((1, 1, N_GATES, CHUNK), lambda b, s: (b, idx(s), 0, 0))]

    out_f = pl.BlockSpec((1, CHUNK, 2 * MIX_W), lambda b, s: (b, jnp.maximum(s - n_cc, 0), 0))
    out_b = pl.BlockSpec((1, CHUNK, 2 * MIX_W),
                         lambda b, s: (b, jnp.minimum(n_ch - 1 - s, n_ch - 1 - n_cc), 0))
    args_dir = [p, p, p, p, qk, gates, gates_rows]
    return pl.pallas_call(
        functools.partial(_mix_kernel, n_cc),
        grid=(bsz, n_ch),
        in_specs=[pl.BlockSpec((2, 2, MIX_W), lambda b, s: (0, 0, 0))]
        + dir_specs(f_idx, COL_HF0) + dir_specs(b_idx, COL_HF1),
        out_specs=[out_f, out_b],
        out_shape=[jax.ShapeDtypeStruct((bsz, n_lat, 2 * MIX_W), F32)] * 2,
        scratch_shapes=[pltpu.VMEM((2, N_HEADS, D_HEAD, D_HEAD), F32),
                        pltpu.VMEM((2, N_HEADS, D_HEAD, D_HEAD), F32),
                        pltpu.VMEM((2, N_HEADS, 1, D_HEAD), F32),
                        pltpu.VMEM((2, N_HEADS, 1, LANES), F32)],
        compiler_params=_cparams(("arbitrary", "arbitrary")),
        name="mix",
    )(lb_logits, *args_dir, *args_dir)


def _out_kernel(of_ref, ob_ref, hg_ref, mo_ref, x_ref, mod_ref, w_ref, hgn_ref, mln_ref, lng_ref, lnb_ref,
                x1_ref, h2t_ref):
    o = of_ref[0] + ob_ref[0]
    hg = hg_ref[0]
    mo = mo_ref[0]
    parts = []
    for h in range(N_HEADS):
        sl = slice(h * D_HEAD, (h + 1) * D_HEAD)
        oh = o[:, sl]
        y = oh * lax.rsqrt(jnp.mean(oh * oh, axis=-1, keepdims=True) + LN_EPS) * hgn_ref[:, sl]
        parts.append(y * _silu(hg[:, sl]))
    for h in range(N_HEADS):
        sl = slice(h * D_HEAD, (h + 1) * D_HEAD)
        oh = o[:, MIX_W + h * D_HEAD:MIX_W + (h + 1) * D_HEAD]
        parts.append(_ln(oh) * mln_ref[:, sl] * _sigmoid(mo[:, sl]))
    mix = jnp.concatenate(parts, axis=-1).astype(BF16)
    y = _mm(mix, w_ref[...])
    g1 = mod_ref[0, :, 0:D_MODEL]
    sh2 = mod_ref[0, :, D_MODEL:2 * D_MODEL]
    sc2 = mod_ref[0, :, 2 * D_MODEL:3 * D_MODEL]
    x1 = _ln(DEEPNORM_ALPHA * x_ref[0] + g1 * y) * lng_ref[...] + lnb_ref[...]
    x1_ref[0] = x1
    h2 = _ln(x1) * (1.0 + sc2) + sh2
    h2t_ref[...] = h2.T


def _readout_project(o_f, o_b, p, x, mod2, w_out, hg_norm_g, ml_norm_g, ln1_g, ln1_b, n_ctx):
    bsz, n_lat, _ = x.shape
    n_tiles = n_lat // ROW_TILE
    off = n_ctx // ROW_TILE
    row = lambda a: a.reshape(1, -1)
    const2 = lambda b, t: (0, 0)
    return pl.pallas_call(
        _out_kernel,
        grid=(bsz, n_tiles),
        in_specs=[pl.BlockSpec((1, ROW_TILE, 2 * MIX_W), lambda b, t: (b, t, 0)),
                  pl.BlockSpec((1, ROW_TILE, 2 * MIX_W), lambda b, t: (b, t, 0)),
                  pl.BlockSpec((1, ROW_TILE, MIX_W), lambda b, t: (b, t + off, COL_HG)),
                  pl.BlockSpec((1, ROW_TILE, MIX_W), lambda b, t: (b, t + off, COL_MO)),
                  pl.BlockSpec((1, ROW_TILE, D_MODEL), lambda b, t: (b, t, 0)),
                  pl.BlockSpec((1, 1, 3 * D_MODEL), lambda b, t: (b, 0, 0)),
                  pl.BlockSpec((2 * MIX_W, D_MODEL), const2),
                  pl.BlockSpec((1, MIX_W), const2),
                  pl.BlockSpec((1, MIX_W), const2),
                  pl.BlockSpec((1, D_MODEL), const2),
                  pl.BlockSpec((1, D_MODEL), const2)],
        out_specs=[pl.BlockSpec((1, ROW_TILE, D_MODEL), lambda b, t: (b, t, 0)),
                   pl.BlockSpec((D_MODEL, ROW_TILE), lambda b, t: (0, b * n_tiles + t))],
        out_shape=[jax.ShapeDtypeStruct((bsz, n_lat, D_MODEL), F32),
                   jax.ShapeDtypeStruct((D_MODEL, bsz * n_lat), F32)],
        compiler_params=_cparams(("arbitrary", "arbitrary")),
        name="readout",
    )(o_f, o_b, p, p, x, mod2, w_out.astype(BF16), row(hg_norm_g), row(ml_norm_g), row(ln1_g), row(ln1_b))


def _extract_top(s, n_rows):
    ridx = lax.broadcasted_iota(jnp.int32, s.shape, 0).astype(F32)
    rank = jnp.full(s.shape, float(PEER_TOPK), F32)
    vals = []
    for it in range(PEER_TOPK):
        m = jnp.max(s, axis=0, keepdims=True)
        first = jnp.min(jnp.where(s == m, ridx, float(n_rows)), axis=0, keepdims=True)
        sel = ridx == first
        rank = jnp.where(sel, float(it), rank)
        s = jnp.where(sel, -jnp.inf, s)
        vals.append(m)
    return rank, vals


_CELLS = [(a, b) for a in range(PEER_TOPK) for b in range(PEER_TOPK) if (a + 1) * (b + 1) <= PEER_TOPK]
_N_CELLS_PAD = -(-len(_CELLS) // 8) * 8


def _sel_kernel(h_ref, wq_hi_ref, wq_lo_ref, k_hi_ref, k_lo_ref, r2_ref, ci_ref, a_ref, b_ref):
    h2t = h_ref[...]
    tt = h2t.shape[1]
    h_hi, h_lo = _split(h2t, 2)
    qt = (_mm(wq_hi_ref[...], h_hi) + (_mm(wq_lo_ref[...], h_hi) + _mm(wq_hi_ref[...], h_lo)))
    half = PEER_DQ // 2
    for hd in range(PEER_HEADS):
        s = []
        for p_ in range(2):
            qh = qt[hd * PEER_DQ + p_ * half: hd * PEER_DQ + (p_ + 1) * half, :]
            q_hi, q_lo = _split(qh, 2)
            k_hi = k_hi_ref[hd, p_]
            k_lo = k_lo_ref[hd, p_]
            s.append(_mm(k_hi, q_hi) + (_mm(k_lo, q_hi) + _mm(k_hi, q_lo)))
        rank1, v1 = _extract_top(s[0], PEER_NKEYS)
        rank2, v2 = _extract_top(s[1], PEER_NKEYS)
        rows = [v1[a] + v2[b] for (a, b) in _CELLS]
        rows += [jnp.full((1, tt), -jnp.inf, F32)] * (_N_CELLS_PAD - len(_CELLS))
        cand = jnp.concatenate(rows, axis=0)
        crank, _ = _extract_top(cand, _N_CELLS_PAD)
        chosen = crank < float(PEER_TOPK)
        top = v1[0] + v2[0]
        z = jnp.sum(jnp.where(chosen, jnp.exp(cand - top), 0.0), axis=0, keepdims=True)
        ci = jnp.zeros_like(rank1)
        pos = 0
        for a in range(PEER_TOPK):
            nb = PEER_TOPK // (a + 1)
            cnt_a = jnp.sum(jnp.where(chosen[pos:pos + nb, :], 1.0, 0.0), axis=0, keepdims=True)
            pos += nb
            ci = jnp.where(rank1 == float(a), cnt_a, ci)
        r2_ref[hd] = rank2
        ci_ref[hd] = ci
        a_ref[hd] = jnp.exp(s[0] - v1[0])
        b_ref[hd] = jnp.exp(s[1] - v2[0]) / z


def _peer_select(h2t, peer_wq, peer_keys):
    n_tok = h2t.shape[1]
    wqt = peer_wq.T
    wq_hi = wqt.astype(BF16)
    wq_lo = (wqt - wq_hi.astype(F32)).astype(BF16)
    k_hi = peer_keys.astype(BF16)
    k_lo = (peer_keys - k_hi.astype(F32)).astype(BF16)
    dq_all = PEER_HEADS * PEER_DQ
    out_spec = pl.BlockSpec((PEER_HEADS, PEER_NKEYS, SEL_TILE), lambda t: (0, 0, t))
    out_shape = jax.ShapeDtypeStruct((PEER_HEADS, PEER_NKEYS, n_tok), F32)
    return pl.pallas_call(
        _sel_kernel,
        grid=(n_tok // SEL_TILE,),
        in_specs=[pl.BlockSpec((D_MODEL, SEL_TILE), lambda t: (0, t)),
                  pl.BlockSpec((dq_all, D_MODEL), lambda t: (0, 0)),
                  pl.BlockSpec((dq_all, D_MODEL), lambda t: (0, 0)),
                  pl.BlockSpec((PEER_HEADS, 2, PEER_NKEYS, PEER_DQ // 2), lambda t: (0, 0, 0, 0)),
                  pl.BlockSpec((PEER_HEADS, 2, PEER_NKEYS, PEER_DQ // 2), lambda t: (0, 0, 0, 0))],
        out_specs=[out_spec] * 4,
        out_shape=[out_shape] * 4,
        compiler_params=_cparams(("arbitrary",)),
        name="peersel",
    )(h2t, wq_hi, wq_lo, k_hi, k_lo)


def _peer_kernel(n_e, h_ref, r2_ref, ci_ref, a_ref, b_ref, u_ref, vt_ref, x1_ref, g2_ref, lng_ref, lnb_ref,
                 o_ref, acc_ref, wa_ref):
    e = pl.program_id(1)

    @pl.when(e == 0)
    def _():
        acc_ref[...] = jnp.zeros_like(acc_ref)

    pre = _mm(u_ref[...], h_ref[...].astype(BF16))
    act = _gelu(pre)
    per = PEER_ET // PEER_NKEYS
    for ii in range(per):
        i = e * per + ii
        w = jnp.zeros((PEER_NKEYS, PEER_TT), F32)
        for hd in range(PEER_HEADS):
            ci_row = ci_ref[hd, pl.ds(i, 1), :]
            a_row = a_ref[hd, pl.ds(i, 1), :]
            w = w + jnp.where(r2_ref[hd] < ci_row, b_ref[hd] * a_row, 0.0)
        wa_ref[pl.ds(ii * PEER_NKEYS, PEER_NKEYS), :] = (
            w * act[ii * PEER_NKEYS:(ii + 1) * PEER_NKEYS, :]).astype(BF16)
    acc_ref[...] += _mm(vt_ref[...], wa_ref[...])

    @pl.when(e == n_e - 1)
    def _():
        y = acc_ref[...].T
        o_ref[0] = _ln(DEEPNORM_ALPHA * x1_ref[0] + g2_ref[0] * y) * lng_ref[...] + lnb_ref[...]


def _peer_dense(h2t, r2, ci, a_fac, b_fac, u_bf, vt_bf, x1, g2, ln2_g, ln2_b):
    bsz, n_lat, _ = x1.shape
    n_tok = bsz * n_lat
    n_t = n_tok // PEER_TT
    per_b = n_lat // PEER_TT
    n_e = u_bf.shape[0] // PEER_ET
    tok_spec = pl.BlockSpec((PEER_HEADS, PEER_NKEYS, PEER_TT), lambda t, e: (0, 0, t))
    const2 = lambda t, e: (0, 0)
    return pl.pallas_call(
        functools.partial(_peer_kernel, n_e),
        grid=(n_t, n_e),
        in_specs=[pl.BlockSpec((D_MODEL, PEER_TT), lambda t, e: (0, t)),
                  tok_spec, tok_spec, tok_spec, tok_spec,
                  pl.BlockSpec((PEER_ET, D_MODEL), lambda t, e: (e, 0)),
                  pl.BlockSpec((D_MODEL, PEER_ET), lambda t, e: (0, e)),
                  pl.BlockSpec((1, PEER_TT, D_MODEL), lambda t, e: (t // per_b, t % per_b, 0)),
                  pl.BlockSpec((1, 1, D_MODEL), lambda t, e: (t // per_b, 0, 0)),
                  pl.BlockSpec((1, D_MODEL), const2),
                  pl.BlockSpec((1, D_MODEL), const2)],
        out_specs=pl.BlockSpec((1, PEER_TT, D_MODEL), lambda t, e: (t // per_b, t % per_b, 0)),
        out_shape=jax.ShapeDtypeStruct((bsz, n_lat, D_MODEL), F32),
        scratch_shapes=[pltpu.VMEM((D_MODEL, PEER_TT), F32),
                        pltpu.VMEM((PEER_ET, PEER_TT), BF16)],
        compiler_params=_cparams(("arbitrary", "arbitrary")),
        name="peer",
    )(h2t, r2, ci, a_fac, b_fac, u_bf, vt_bf, x1, g2, ln2_g.reshape(1, -1), ln2_b.reshape(1, -1))


def kernel(x, c, ctx, c_ctx, w_mod, b_mod, w_in, hg_lb_logits, hg_norm_g, ml_conv_w, ml_conv_b, ml_gate_b,
           ml_norm_g, w_out, ln1_g, ln1_b, peer_wq, peer_keys, peer_u, peer_v, ln2_g, ln2_b):
    bsz, n_lat, _ = x.shape
    n_ctx = ctx.shape[1]
    assert n_ctx % ROW_TILE == 0 and n_lat % ROW_TILE == 0 and ROW_TILE % CHUNK == 0
    assert w_mod.shape[0] == 1, "single-layer kernel"
    lyr = 0

    n_rows = -(-(bsz + 1) // 8) * 8
    c_all = jnp.zeros((n_rows, D_MODEL), F32).at[:bsz].set(c).at[bsz].set(c_ctx)
    mod = _modulation(c_all, w_mod[lyr], b_mod[lyr])
    mod_l, mod_c = mod[:bsz], mod[bsz]
    n_tiles_ctx = n_ctx // ROW_TILE
    n_tiles = (n_ctx + n_lat) // ROW_TILE
    mod1_l = mod_l[:, None, :2 * D_MODEL]
    mod1_c = jnp.broadcast_to(mod_c[None, None, :2 * D_MODEL], (bsz, 1, 2 * D_MODEL))
    mod1 = jnp.concatenate([jnp.repeat(mod1_c, n_tiles_ctx, axis=1),
                            jnp.repeat(mod1_l, n_tiles - n_tiles_ctx, axis=1)], axis=1)[:, :, None, :]
    mod2 = mod_l[:, None, 2 * D_MODEL:5 * D_MODEL]
    g2 = mod_l[:, None, 5 * D_MODEL:6 * D_MODEL]

    x_all = jnp.concatenate([ctx, x], axis=1)
    w_main = w_in[lyr][:, :D_MAIN].astype(BF16)
    w_gates = w_in[lyr][:, D_MAIN:]
    p, gates, gates_t = _input_projection(x_all, mod1, w_main, w_gates, ml_gate_b[lyr])
    n_ch = (n_ctx + n_lat) // CHUNK
    gates_rows = gates_t.reshape(bsz, N_GATES, n_ch, CHUNK).transpose(0, 2, 1, 3)

    qk = _qk_conv(p, ml_conv_w[lyr], ml_conv_b[lyr], n_ctx)
    o_f, o_b = _token_mixers(p, qk, gates, gates_rows, hg_lb_logits[:, lyr:lyr + 2], n_ctx)
    x1, h2t = _readout_project(o_f, o_b, p, x, mod2, w_out[lyr], hg_norm_g[lyr], ml_norm_g[lyr],
                               ln1_g[lyr], ln1_b[lyr], n_ctx)
    r2, ci, a_fac, b_fac = _peer_select(h2t, peer_wq[lyr], peer_keys[lyr])
    u_bf = peer_u[lyr].astype(BF16)
    vt_bf = peer_v[lyr].T.astype(BF16)
    return _peer_dense(h2t, r2, ci, a_fac, b_fac, u_bf, vt_bf, x1, g2, ln2_g[lyr], ln2_b[lyr])
```

```python
import functools
import math

import jax
import jax.numpy as jnp
from jax import lax
from jax.experimental import pallas as pl
from jax.experimental.pallas import tpu as pltpu

F32 = jnp.float32
BF16 = jnp.bfloat16

D_MODEL = 1024
CHUNK = 64
GRID_W = 64
N_HEADS = 4
D_HEAD = 128
MIX_W = N_HEADS * D_HEAD
N_GATES = 4 * N_HEADS
D_MAIN = 9 * MIX_W
PEER_HEADS = 8
PEER_NKEYS = 128
PEER_TOPK = 16
PEER_DQ = 256
LN_EPS = 1e-6
DEEPNORM_ALPHA = 2.0 ** 0.25
LANES = 128
VMEM_LIMIT = 56 * 1024 * 1024

COL_HQ, COL_HI, COL_HG, COL_HF0, COL_HF1, COL_MQ, COL_MK, COL_MV, COL_MO = range(9)

ROW_TILE = 256
SEL_TILE = 256
PEER_TT = 512
PEER_ET = 1024

NT_DIMS = (((1,), (1,)), ((), ()))
TN_DIMS = (((0,), (0,)), ((), ()))


def _cparams(sem):
    return pltpu.CompilerParams(dimension_semantics=sem, vmem_limit_bytes=VMEM_LIMIT)


def _split(a, n):
    parts = []
    r = a
    for k in range(n):
        p = r.astype(BF16)
        parts.append(p)
        if k + 1 < n:
            r = r - p.astype(F32)
    return parts


def _mm(a, b, dims=None):
    if dims is None:
        return jnp.dot(a, b, preferred_element_type=F32)
    return lax.dot_general(a, b, dims, preferred_element_type=F32)


def _mm_bf16(a, b, dims=None):
    return _mm(a.astype(BF16), b.astype(BF16), dims)


def _mm_x3(a, b, dims=None):
    a_hi, a_lo = _split(a, 2)
    b_hi, b_lo = _split(b, 2)
    return _mm(a_hi, b_hi, dims) + (_mm(a_hi, b_lo, dims) + _mm(a_lo, b_hi, dims))


def _mm_x3_pre(a, b_hi, b_lo, dims=None):
    a_hi, a_lo = _split(a, 2)
    return _mm(a_hi, b_hi, dims) + (_mm(a_hi, b_lo, dims) + _mm(a_lo, b_hi, dims))


def _mm_exact_left(m_bf16, x):
    x0, x1, x2 = _split(x, 3)
    return _mm(m_bf16, x0) + (_mm(m_bf16, x1) + _mm(m_bf16, x2))


def _mm_exact_right(x, m_bf16):
    x0, x1, x2 = _split(x, 3)
    return _mm(x0, m_bf16) + (_mm(x1, m_bf16) + _mm(x2, m_bf16))


def _sigmoid(x):
    return 1.0 / (1.0 + jnp.exp(-x))


def _silu(x):
    return x * _sigmoid(x)


def _log_sigmoid(x):
    return jnp.minimum(x, 0.0) - jnp.log(1.0 + jnp.exp(-jnp.abs(x)))


def _ln(x):
    mu = jnp.mean(x, axis=-1, keepdims=True)
    xc = x - mu
    var = jnp.mean(xc * xc, axis=-1, keepdims=True)
    return xc * lax.rsqrt(var + LN_EPS)


def _gelu(x):
    return 0.5 * x * (1.0 + lax.erf(x * (2.0 ** -0.5)))


def _mod_kernel(c_ref, w_ref, b_ref, o_ref):
    s = _silu(c_ref[...])
    o_ref[...] = _mm_x3(s, w_ref[...]) + b_ref[...]


def _modulation(c_all, w_mod, b_mod):
    n = c_all.shape[0]
    d_out = w_mod.shape[1]
    blk = 1024
    return pl.pallas_call(
        _mod_kernel,
        grid=(d_out // blk,),
        in_specs=[pl.BlockSpec((n, D_MODEL), lambda j: (0, 0)),
                  pl.BlockSpec((D_MODEL, blk), lambda j: (0, j)),
                  pl.BlockSpec((1, blk), lambda j: (0, j))],
        out_specs=pl.BlockSpec((n, blk), lambda j: (0, j)),
        out_shape=jax.ShapeDtypeStruct((n, d_out), F32),
        compiler_params=_cparams(("arbitrary",)),
        name="mod",
    )(c_all, w_mod, b_mod.reshape(1, d_out))


def _inproj_kernel(x_ref, mod_ref, w_ref, wg_hi_ref, wg_lo_ref, wgt_hi_ref, wgt_lo_ref, gb_ref, gbt_ref,
                   p_ref, g_ref, gt_ref):
    x = x_ref[0]
    shift = mod_ref[0, 0, :, :D_MODEL]
    scale = mod_ref[0, 0, :, D_MODEL:]
    h = _ln(x) * (1.0 + scale) + shift
    p_ref[0] = _mm(h.astype(BF16), w_ref[...])
    h_hi, h_lo = _split(h, 2)
    g = _mm(h_hi, wg_hi_ref[...]) + (_mm(h_hi, wg_lo_ref[...]) + _mm(h_lo, wg_hi_ref[...]))
    g_ref[0] = g + gb_ref[...]
    gt = (_mm(wgt_hi_ref[...], h_hi, NT_DIMS)
          + (_mm(wgt_lo_ref[...], h_hi, NT_DIMS) + _mm(wgt_hi_ref[...], h_lo, NT_DIMS)))
    gt_ref[0] = gt + gbt_ref[...]


def _input_projection(x_all, mod1, w_main, wg, gate_b):
    bsz, t_all, _ = x_all.shape
    n_tiles = t_all // ROW_TILE
    wg_pad = jnp.zeros((D_MODEL, LANES), F32).at[:, :N_GATES].set(wg)
    wg_hi = wg_pad.astype(BF16)
    wg_lo = (wg_pad - wg_hi.astype(F32)).astype(BF16)
    wgt = wg.T
    wgt_hi = wgt.astype(BF16)
    wgt_lo = (wgt - wgt_hi.astype(F32)).astype(BF16)
    gb = jnp.zeros((1, LANES), F32).at[0, :N_GATES].set(gate_b)
    gbt = gate_b.reshape(N_GATES, 1)
    const2 = lambda b, t: (0, 0)
    return pl.pallas_call(
        _inproj_kernel,
        grid=(bsz, n_tiles),
        in_specs=[pl.BlockSpec((1, ROW_TILE, D_MODEL), lambda b, t: (b, t, 0)),
                  pl.BlockSpec((1, 1, 1, 2 * D_MODEL), lambda b, t: (b, t, 0, 0)),
                  pl.BlockSpec((D_MODEL, D_MAIN), const2),
                  pl.BlockSpec((D_MODEL, LANES), const2),
                  pl.BlockSpec((D_MODEL, LANES), const2),
                  pl.BlockSpec((N_GATES, D_MODEL), const2),
                  pl.BlockSpec((N_GATES, D_MODEL), const2),
                  pl.BlockSpec((1, LANES), const2),
                  pl.BlockSpec((N_GATES, 1), const2)],
        out_specs=[pl.BlockSpec((1, ROW_TILE, D_MAIN), lambda b, t: (b, t, 0)),
                   pl.BlockSpec((1, ROW_TILE, LANES), lambda b, t: (b, t, 0)),
                   pl.BlockSpec((1, N_GATES, ROW_TILE), lambda b, t: (b, 0, t))],
        out_shape=[jax.ShapeDtypeStruct((bsz, t_all, D_MAIN), F32),
                   jax.ShapeDtypeStruct((bsz, t_all, LANES), F32),
                   jax.ShapeDtypeStruct((bsz, N_GATES, t_all), F32)],
        compiler_params=_cparams(("arbitrary", "arbitrary")),
        name="inproj",
    )(x_all, mod1, w_main, wg_hi, wg_lo, wgt_hi, wgt_lo, gb, gbt)


CONV_PAD = 72
CONV_CB = 256


def _conv_kernel(n_ctx, n_lat, p_ref, w_ref, b_ref, o_ref, pad_c, pad_l):
    cb = pl.program_id(1)
    w = w_ref[...]
    bias = b_ref[...]
    scale = jnp.where(cb >= MIX_W // CONV_CB, D_HEAD ** -0.5, 1.0).astype(F32)

    def finish(y):
        return _silu(y + bias) * scale

    pad_c[...] = jnp.zeros_like(pad_c)
    pad_c[pl.ds(CONV_PAD, n_ctx), :] = p_ref[0, pl.ds(0, n_ctx), :]
    y = jnp.zeros((n_ctx, CONV_CB), F32)
    for dc in range(3):
        y = y + pad_c[pl.ds(CONV_PAD + dc - 1, n_ctx), :] * w[3 + dc:4 + dc, :]
    o_ref[0, pl.ds(0, n_ctx), :] = finish(y)

    pad_l[...] = jnp.zeros_like(pad_l)
    pad_l[pl.ds(CONV_PAD, n_lat), :] = p_ref[0, pl.ds(n_ctx, n_lat), :]
    col = lax.broadcasted_iota(jnp.int32, (n_lat, CONV_CB), 0) & (GRID_W - 1)
    y = jnp.zeros((n_lat, CONV_CB), F32)
    for dr in range(3):
        for dc in range(3):
            shift = (dr - 1) * GRID_W + (dc - 1)
            t = pad_l[pl.ds(CONV_PAD + shift, n_lat), :] * w[3 * dr + dc:3 * dr + dc + 1, :]
            if dc == 0:
                t = jnp.where(col >= 1, t, 0.0)
            elif dc == 2:
                t = jnp.where(col <= GRID_W - 2, t, 0.0)
            y = y + t
    o_ref[0, pl.ds(n_ctx, n_lat), :] = finish(y)


def _qk_conv(p, conv_w, conv_b, n_ctx):
    bsz, t_all, _ = p.shape
    n_lat = t_all - n_ctx
    n_cb = 2 * MIX_W // CONV_CB
    first = COL_MQ * MIX_W // CONV_CB
    w9 = conv_w.reshape(9, 2 * MIX_W)
    return pl.pallas_call(
        functools.partial(_conv_kernel, n_ctx, n_lat),
        grid=(bsz, n_cb),
        in_specs=[pl.BlockSpec((1, t_all, CONV_CB), lambda b, c: (b, 0, first + c)),
                  pl.BlockSpec((9, CONV_CB), lambda b, c: (0, c)),
                  pl.BlockSpec((1, CONV_CB), lambda b, c: (0, c))],
        out_specs=pl.BlockSpec((1, t_all, CONV_CB), lambda b, c: (b, 0, c)),
        out_shape=jax.ShapeDtypeStruct((bsz, t_all, 2 * MIX_W), F32),
        scratch_shapes=[pltpu.VMEM((n_ctx + 2 * CONV_PAD, CONV_CB), F32),
                        pltpu.VMEM((n_lat + 2 * CONV_PAD, CONV_CB), F32)],
        compiler_params=_cparams(("arbitrary", "arbitrary")),
        name="qkconv",
    )(p, w9, conv_b.reshape(1, 2 * MIX_W))


def _tri(rev):
    r = lax.broadcasted_iota(jnp.int32, (CHUNK, CHUNK), 0)
    c = lax.broadcasted_iota(jnp.int32, (CHUNK, CHUNK), 1)
    return (c >= r) if rev else (c <= r)


def _hgrn_dir(d, rev, want_out, q_ref, v_ref, z_ref, lb, st_ref, o_ref):
    v = v_ref[0]
    z = z_ref[0]
    sig = _sigmoid(z)
    nsig = _sigmoid(-z)
    log_f = jnp.log(lb + (1.0 - lb) * sig)
    kk = (1.0 - lb) * nsig
    mask = _tri(rev)
    tri = jnp.where(mask, 1.0, 0.0).astype(BF16)
    b = _mm_exact_left(tri, log_f)
    last = 0 if rev else CHUNK - 1
    mid = CHUNK // 2 - 1 if rev else CHUNK // 2
    b_last = b[last:last + 1, :]
    b_mid = b[mid:mid + 1, :]
    kdec = (kk * jnp.exp(b_last - b)).astype(BF16)
    e_last = jnp.exp(b_last)
    vb = v.astype(BF16)

    @pl.when(want_out)
    def _():
        q = q_ref[0]
        qd = (q * jnp.exp(b - b_mid)).astype(BF16)
        kd = (kk * jnp.exp(b_mid - b)).astype(BF16)
        qe = (q * jnp.exp(b)).astype(BF16)
        for h in range(N_HEADS):
            sl = slice(h * D_HEAD, (h + 1) * D_HEAD)
            sc = _mm(qd[:, sl], kd[:, sl], NT_DIMS)
            sc = jnp.where(mask, sc, 0.0).astype(BF16)
            o = _mm(sc, vb[:, sl]) + _mm(qe[:, sl], st_ref[d, h].astype(BF16), NT_DIMS)
            o_ref[0, :, sl] = o

    for h in range(N_HEADS):
        sl = slice(h * D_HEAD, (h + 1) * D_HEAD)
        st_ref[d, h] = st_ref[d, h] * e_last[:, sl] + _mm(vb[:, sl], kdec[:, sl], TN_DIMS)


def _mlstm_dir(d, rev, want_out, qk_ref, v_ref, gc_ref, gr_ref, c_ref, n_ref, m_ref, o_ref):
    mask = _tri(rev)
    tri = jnp.where(mask, 1.0, 0.0).astype(BF16)
    tri_t = jnp.where(_tri(not rev), 1.0, 0.0).astype(BF16)
    gc = gc_ref[0]
    gr = gr_ref[0, 0]
    a_cols = _mm_exact_left(tri, _log_sigmoid(gc))
    a_rows = _mm_exact_right(_log_sigmoid(gr), tri_t)
    last = 0 if rev else CHUNK - 1
    qk = qk_ref[0]
    v = v_ref[0]
    vb = v.astype(BF16)
    for h in range(N_HEADS):
        ic = d * N_HEADS + h
        fc = 2 * N_HEADS + d * N_HEADS + h
        sl = slice(h * D_HEAD, (h + 1) * D_HEAD)
        q_h = qk[:, h * D_HEAD:(h + 1) * D_HEAD]
        k_h = qk[:, MIX_W + h * D_HEAD:MIX_W + (h + 1) * D_HEAD]
        v_h = vb[:, sl]
        a_c = a_cols[:, fc:fc + 1]
        a_r = a_rows[fc:fc + 1, :]
        li_c = gc[:, ic:ic + 1]
        li_r = gr[ic:ic + 1, :]
        a_last = a_r[:, last:last + 1]
        m_prev = m_ref[d, h][:, 0:1]
        c_prev = c_ref[d, h]
        n_prev = n_ref[d, h]

        @pl.when(want_out)
        def _():
            m_inter = a_c + m_prev
            log_d = jnp.where(mask, a_c - a_r + li_r, -jnp.inf)
            m_t = jnp.maximum(jnp.max(log_d, axis=1, keepdims=True), m_inter)
            s_qk = _mm_bf16(q_h, k_h, NT_DIMS)
            w_intra = s_qk * jnp.exp(log_d - m_t)
            w_inter = jnp.exp(m_inter - m_t)
            num = _mm(w_intra.astype(BF16), v_h) + w_inter * _mm_bf16(q_h, c_prev)
            qn = jnp.sum(q_h * n_prev, axis=1, keepdims=True)
            den = jnp.sum(w_intra, axis=1, keepdims=True) + w_inter * qn
            o_ref[0, :, MIX_W + h * D_HEAD:MIX_W + (h + 1) * D_HEAD] = (
                num / jnp.maximum(jnp.abs(den), jnp.exp(-m_t)))

        g_c = a_last - a_c + li_c
        m_loc = jnp.max(g_c, axis=0, keepdims=True)
        wk = jnp.exp(g_c - m_loc) * k_h
        upd_c = _mm(wk.astype(BF16), v_h, TN_DIMS)
        upd_n = jnp.sum(wk, axis=0, keepdims=True)
        m_new = jnp.maximum(a_last + m_prev, m_loc)
        s_old = jnp.exp(a_last + m_prev - m_new)
        s_new = jnp.exp(m_loc - m_new)
        c_ref[d, h] = s_old * c_prev + s_new * upd_c
        n_ref[d, h] = s_old * n_prev + s_new * upd_n
        m_ref[d, h] = jnp.broadcast_to(m_new, (1, LANES))


def _mix_kernel(n_cc, lbl_ref,
                hq_f, hi_f, hf_f, mv_f, qk_f, gc_f, gr_f,
                hq_b, hi_b, hf_b, mv_b, qk_b, gc_b, gr_b,
                of_ref, ob_ref, st_ref, c_ref, n_ref, m_ref):
    s = pl.program_id(1)

    @pl.when(s == 0)
    def _():
        st_ref[...] = jnp.zeros_like(st_ref)
        c_ref[...] = jnp.zeros_like(c_ref)
        n_ref[...] = jnp.zeros_like(n_ref)
        m_ref[...] = jnp.zeros_like(m_ref)

    want_out = s >= n_cc
    for d, rev, (hq, hi, hf, mv, qk, gc, gr, o_ref) in (
            (0, False, (hq_f, hi_f, hf_f, mv_f, qk_f, gc_f, gr_f, of_ref)),
            (1, True, (hq_b, hi_b, hf_b, mv_b, qk_b, gc_b, gr_b, ob_ref))):
        logits = lbl_ref[d]
        mx = jnp.max(logits, axis=0, keepdims=True)
        ex = jnp.exp(logits - mx)
        lb = ex[0:1, :] / jnp.sum(ex, axis=0, keepdims=True)
        _hgrn_dir(d, rev, want_out, hq, hi, hf, lb, st_ref, o_ref)
        _mlstm_dir(d, rev, want_out, qk, mv, gc, gr, c_ref, n_ref, m_ref, o_ref)


def _token_mixers(p, qk, gates, gates_rows, lb_logits, n_ctx):
    bsz, t_all, _ = p.shape
    n_lat = t_all - n_ctx
    n_cc = n_ctx // CHUNK
    n_ch = t_all // CHUNK

    def f_idx(s):
        return s

    def b_idx(s):
        return jnp.where(s < n_cc, n_cc - 1 - s, n_ch - 1 + n_cc - s)

    def pspec(col, idx):
        return pl.BlockSpec((1, CHUNK, MIX_W), lambda b, s: (b, idx(s), col))

    def dir_specs(idx, col_f):
        return [pspec(COL_HQ, idx), pspec(COL_HI, idx), pspec(col_f, idx), pspec(COL_MV, idx),
                pl.BlockSpec((1, CHUNK, 2 * MIX_W), lambda b, s: (b, idx(s), 0)),
                pl.BlockSpec((1, CHUNK, LANES), lambda b, s: (b, idx(s), 0)),
                pl.BlockSpec((1, 1, N_GATES, CHUNK), lambda b, s: (b, idx(s), 0, 0))]

    out_f = pl.BlockSpec((1, CHUNK, 2 * MIX_W), lambda b, s: (b, jnp.maximum(s - n_cc, 0), 0))
    out_b = pl.BlockSpec((1, CHUNK, 2 * MIX_W),
                         lambda b, s: (b, jnp.minimum(n_ch - 1 - s, n_ch - 1 - n_cc), 0))
    args_dir = [p, p, p, p, qk, gates, gates_rows]
    return pl.pallas_call(
        functools.partial(_mix_kernel, n_cc),
        grid=(bsz, n_ch),
        in_specs=[pl.BlockSpec((2, 2, MIX_W), lambda b, s: (0, 0, 0))]
        + dir_specs(f_idx, COL_HF0) + dir_specs(b_idx, COL_HF1),
        out_specs=[out_f, out_b],
        out_shape=[jax.ShapeDtypeStruct((bsz, n_lat, 2 * MIX_W), F32)] * 2,
        scratch_shapes=[pltpu.VMEM((2, N_HEADS, D_HEAD, D_HEAD), F32),
                        pltpu.VMEM((2, N_HEADS, D_HEAD, D_HEAD), F32),
                        pltpu.VMEM((2, N_HEADS, 1, D_HEAD), F32),
                        pltpu.VMEM((2, N_HEADS, 1, LANES), F32)],
        compiler_params=_cparams(("arbitrary", "arbitrary")),
        name="mix",
    )(lb_logits, *args_dir, *args_dir)


def _out_kernel(of_ref, ob_ref, hg_ref, mo_ref, x_ref, mod_ref, w_ref, hgn_ref, mln_ref, lng_ref, lnb_ref,
                x1_ref, h2t_ref, h2t_bf_ref):
    o = of_ref[0] + ob_ref[0]
    hg = hg_ref[0]
    mo = mo_ref[0]
    parts = []
    for h in range(N_HEADS):
        sl = slice(h * D_HEAD, (h + 1) * D_HEAD)
        oh = o[:, sl]
        y = oh * lax.rsqrt(jnp.mean(oh * oh, axis=-1, keepdims=True) + LN_EPS) * hgn_ref[:, sl]
        parts.append(y * _silu(hg[:, sl]))
    for h in range(N_HEADS):
        sl = slice(h * D_HEAD, (h + 1) * D_HEAD)
        oh = o[:, MIX_W + h * D_HEAD:MIX_W + (h + 1) * D_HEAD]
        parts.append(_ln(oh) * mln_ref[:, sl] * _sigmoid(mo[:, sl]))
    mix = jnp.concatenate(parts, axis=-1).astype(BF16)
    y = _mm(mix, w_ref[...])
    g1 = mod_ref[0, :, 0:D_MODEL]
    sh2 = mod_ref[0, :, D_MODEL:2 * D_MODEL]
    sc2 = mod_ref[0, :, 2 * D_MODEL:3 * D_MODEL]
    x1 = _ln(DEEPNORM_ALPHA * x_ref[0] + g1 * y) * lng_ref[...] + lnb_ref[...]
    x1_ref[0] = x1
    h2t = (_ln(x1) * (1.0 + sc2) + sh2).T
    h2t_ref[...] = h2t
    h2t_bf_ref[...] = h2t.astype(BF16)


def _readout_project(o_f, o_b, p, x, mod2, w_out, hg_norm_g, ml_norm_g, ln1_g, ln1_b, n_ctx):
    bsz, n_lat, _ = x.shape
    n_tiles = n_lat // ROW_TILE
    off = n_ctx // ROW_TILE
    row = lambda a: a.reshape(1, -1)
    const2 = lambda b, t: (0, 0)
    return pl.pallas_call(
        _out_kernel,
        grid=(bsz, n_tiles),
        in_specs=[pl.BlockSpec((1, ROW_TILE, 2 * MIX_W), lambda b, t: (b, t, 0)),
                  pl.BlockSpec((1, ROW_TILE, 2 * MIX_W), lambda b, t: (b, t, 0)),
                  pl.BlockSpec((1, ROW_TILE, MIX_W), lambda b, t: (b, t + off, COL_HG)),
                  pl.BlockSpec((1, ROW_TILE, MIX_W), lambda b, t: (b, t + off, COL_MO)),
                  pl.BlockSpec((1, ROW_TILE, D_MODEL), lambda b, t: (b, t, 0)),
                  pl.BlockSpec((1, 1, 3 * D_MODEL), lambda b, t: (b, 0, 0)),
                  pl.BlockSpec((2 * MIX_W, D_MODEL), const2),
                  pl.BlockSpec((1, MIX_W), const2),
                  pl.BlockSpec((1, MIX_W), const2),
                  pl.BlockSpec((1, D_MODEL), const2),
                  pl.BlockSpec((1, D_MODEL), const2)],
        out_specs=[pl.BlockSpec((1, ROW_TILE, D_MODEL), lambda b, t: (b, t, 0)),
                   pl.BlockSpec((D_MODEL, ROW_TILE), lambda b, t: (0, b * n_tiles + t)),
                   pl.BlockSpec((D_MODEL, ROW_TILE), lambda b, t: (0, b * n_tiles + t))],
        out_shape=[jax.ShapeDtypeStruct((bsz, n_lat, D_MODEL), F32),
                   jax.ShapeDtypeStruct((D_MODEL, bsz * n_lat), F32),
                   jax.ShapeDtypeStruct((D_MODEL, bsz * n_lat), BF16)],
        compiler_params=_cparams(("arbitrary", "arbitrary")),
        name="readout",
    )(o_f, o_b, p, p, x, mod2, w_out.astype(BF16), row(hg_norm_g), row(ml_norm_g), row(ln1_g), row(ln1_b))


def _extract_exact(s):
    ridx = lax.broadcasted_iota(jnp.int32, s.shape, 0).astype(F32)
    rank = jnp.full(s.shape, float(PEER_TOPK), F32)
    vals = []
    for it in range(PEER_TOPK):
        m = jnp.max(s, axis=0, keepdims=True)
        first = jnp.min(jnp.where(s == m, ridx, float(s.shape[0])), axis=0, keepdims=True)
        hit = ridx == first
        rank = jnp.where(hit, float(it), rank)
        s = jnp.where(hit, -jnp.inf, s)
        vals.append(m)
    return rank, vals


def _extract_fast(s):
    rank = jnp.full(s.shape, float(PEER_TOPK), F32)
    vals = []
    for it in range(PEER_TOPK):
        m = jnp.max(s, axis=0, keepdims=True)
        hit = s == m
        rank = jnp.where(hit, float(it), rank)
        s = jnp.where(hit, -jnp.inf, s)
        vals.append(m)
    return rank, vals


def _count_true(mask):
    return jnp.sum(jnp.where(mask, 1.0, 0.0), axis=0, keepdims=True)


def _cand_grid(v1, v2):
    v1a = jnp.concatenate(v1, axis=0)
    v2a = jnp.concatenate(v2, axis=0)
    row = lax.broadcasted_iota(jnp.int32, (8, v1a.shape[1]), 0)
    blocks = [v1a[0:1] + v2a]
    for a in range(1, 8):
        nb = PEER_TOPK // (a + 1)
        blk = v1a[a:a + 1] + v2a[0:8]
        blocks.append(blk if nb >= 8 else jnp.where(row < nb, blk, -jnp.inf))
    blocks.append(v1a[8:16] + v2a[0:1])
    return jnp.concatenate(blocks, axis=0)


def _cand_counts(chosen):
    cnt = [_count_true(chosen[0:PEER_TOPK])]
    for a in range(1, 8):
        cnt.append(_count_true(chosen[PEER_TOPK + 8 * (a - 1):PEER_TOPK + 8 * a]))
    base = PEER_TOPK + 8 * 7
    for r in range(8):
        cnt.append(jnp.where(chosen[base + r:base + r + 1], 1.0, 0.0))
    return cnt


def _sel_emit(hd, s1, s2, rank1, rank2, v1, v2, cand, chosen, r2_ref, ci_ref, a_ref, b_ref):
    top = v1[0] + v2[0]
    z = jnp.sum(jnp.where(chosen, jnp.exp(cand - top), 0.0), axis=0, keepdims=True)
    ci = jnp.zeros_like(rank1)
    for a, cnt_a in enumerate(_cand_counts(chosen)):
        ci = jnp.where(rank1 == float(a), cnt_a, ci)
    r2_ref[hd] = rank2.astype(BF16)
    ci_ref[hd] = ci
    a_ref[hd] = jnp.exp(s1 - v1[0])
    b_ref[hd] = (jnp.exp(s2 - v2[0]) / z).astype(BF16)


def _sel_kernel(h_ref, wq_hi_ref, wq_lo_ref, k_hi_ref, k_lo_ref, r2_ref, ci_ref, a_ref, b_ref):
    h2t = h_ref[...]
    h_hi, h_lo = _split(h2t, 2)
    qt = (_mm(wq_hi_ref[...], h_hi) + (_mm(wq_lo_ref[...], h_hi) + _mm(wq_hi_ref[...], h_lo)))
    half = PEER_DQ // 2
    for hd in range(PEER_HEADS):
        s = []
        for p_ in range(2):
            qh = qt[hd * PEER_DQ + p_ * half: hd * PEER_DQ + (p_ + 1) * half, :]
            q_hi, q_lo = _split(qh, 2)
            k_hi = k_hi_ref[hd, p_]
            k_lo = k_lo_ref[hd, p_]
            s.append(_mm(k_hi, q_hi) + (_mm(k_lo, q_hi) + _mm(k_hi, q_lo)))
        s1, s2 = s

        rank1, v1 = _extract_fast(s1)
        rank2, v2 = _extract_fast(s2)
        cand = _cand_grid(v1, v2)
        _, cv = _extract_fast(cand)
        chosen = cand >= cv[PEER_TOPK - 1]
        k = float(PEER_TOPK)
        tied = ((_count_true(rank1 < k) != k) | (_count_true(rank2 < k) != k) | (_count_true(chosen) != k))
        any_tied = jnp.max(jnp.where(tied, 1.0, 0.0)) > 0.0
        _sel_emit(hd, s1, s2, rank1, rank2, v1, v2, cand, chosen, r2_ref, ci_ref, a_ref, b_ref)

        @pl.when(any_tied)
        def _():
            rank1, v1 = _extract_exact(s1)
            rank2, v2 = _extract_exact(s2)
            cand = _cand_grid(v1, v2)
            crank, _ = _extract_exact(cand)
            _sel_emit(hd, s1, s2, rank1, rank2, v1, v2, cand, crank < k, r2_ref, ci_ref, a_ref, b_ref)


def _peer_select(h2t, peer_wq, peer_keys):
    n_tok = h2t.shape[1]
    wqt = peer_wq.T
    wq_hi = wqt.astype(BF16)
    wq_lo = (wqt - wq_hi.astype(F32)).astype(BF16)
    k_hi = peer_keys.astype(BF16)
    k_lo = (peer_keys - k_hi.astype(F32)).astype(BF16)
    dq_all = PEER_HEADS * PEER_DQ
    out_spec = pl.BlockSpec((PEER_HEADS, PEER_NKEYS, SEL_TILE), lambda t: (0, 0, t))
    shape = (PEER_HEADS, PEER_NKEYS, n_tok)
    return pl.pallas_call(
        _sel_kernel,
        grid=(n_tok // SEL_TILE,),
        in_specs=[pl.BlockSpec((D_MODEL, SEL_TILE), lambda t: (0, t)),
                  pl.BlockSpec((dq_all, D_MODEL), lambda t: (0, 0)),
                  pl.BlockSpec((dq_all, D_MODEL), lambda t: (0, 0)),
                  pl.BlockSpec((PEER_HEADS, 2, PEER_NKEYS, PEER_DQ // 2), lambda t: (0, 0, 0, 0)),
                  pl.BlockSpec((PEER_HEADS, 2, PEER_NKEYS, PEER_DQ // 2), lambda t: (0, 0, 0, 0))],
        out_specs=[out_spec] * 4,
        out_shape=[jax.ShapeDtypeStruct(shape, BF16), jax.ShapeDtypeStruct(shape, F32),
                   jax.ShapeDtypeStruct(shape, F32), jax.ShapeDtypeStruct(shape, BF16)],
        compiler_params=_cparams(("arbitrary",)),
        name="peersel",
    )(h2t, wq_hi, wq_lo, k_hi, k_lo)


def _rows_bf16(row):
    tile = jnp.broadcast_to(row, (16, row.shape[1])).astype(BF16)
    return jnp.concatenate([tile] * (PEER_NKEYS // 16), axis=0)


def _peer_kernel(n_e, h_ref, r2_ref, b_ref, ci_ref, a_ref, u_ref, vt_ref, x1_ref, g2_ref, lng_ref, lnb_ref,
                 o_ref, acc_ref, pre_ref, wa_ref):
    e = pl.program_id(1)

    @pl.when(e == 0)
    def _():
        acc_ref[...] = jnp.zeros_like(acc_ref)

    pre_ref[...] = _mm(u_ref[...], h_ref[...])
    zero = jnp.zeros((), BF16)
    for ii in range(PEER_ET // PEER_NKEYS):
        w = None
        for hd in range(PEER_HEADS):
            cnt = _rows_bf16(ci_ref[hd, ii:ii + 1, :])
            fac = _rows_bf16(a_ref[hd, ii:ii + 1, :])
            term = jnp.where(r2_ref[hd] < cnt, b_ref[hd] * fac, zero)
            w = term if w is None else w + term
        rows = pl.ds(ii * PEER_NKEYS, PEER_NKEYS)
        wa_ref[rows, :] = w * _gelu(pre_ref[rows, :]).astype(BF16)
    acc_ref[...] += _mm(vt_ref[...], wa_ref[...])

    @pl.when(e == n_e - 1)
    def _():
        y = acc_ref[...].T
        o_ref[0] = _ln(DEEPNORM_ALPHA * x1_ref[0] + g2_ref[0] * y) * lng_ref[...] + lnb_ref[...]


def _peer_dense(h2t_bf, r2, ci, a_fac, b_fac, u_bf, vt_bf, x1, g2, ln2_g, ln2_b):
    bsz, n_lat, _ = x1.shape
    n_tok = bsz * n_lat
    n_t = n_tok // PEER_TT
    per_b = n_lat // PEER_TT
    n_e = u_bf.shape[0] // PEER_ET
    per = PEER_ET // PEER_NKEYS
    assert per == 8, "one sublane tile of half-1 keys per expert step"
    tok_spec = pl.BlockSpec((PEER_HEADS, PEER_NKEYS, PEER_TT), lambda t, e: (0, 0, t))
    key_spec = pl.BlockSpec((PEER_HEADS, per, PEER_TT), lambda t, e: (0, e, t))
    const2 = lambda t, e: (0, 0)
    return pl.pallas_call(
        functools.partial(_peer_kernel, n_e),
        grid=(n_t, n_e),
        in_specs=[pl.BlockSpec((D_MODEL, PEER_TT), lambda t, e: (0, t)),
                  tok_spec, tok_spec, key_spec, key_spec,
                  pl.BlockSpec((PEER_ET, D_MODEL), lambda t, e: (e, 0)),
                  pl.BlockSpec((D_MODEL, PEER_ET), lambda t, e: (0, e)),
                  pl.BlockSpec((1, PEER_TT, D_MODEL), lambda t, e: (t // per_b, t % per_b, 0)),
                  pl.BlockSpec((1, 1, D_MODEL), lambda t, e: (t // per_b, 0, 0)),
                  pl.BlockSpec((1, D_MODEL), const2),
                  pl.BlockSpec((1, D_MODEL), const2)],
        out_specs=pl.BlockSpec((1, PEER_TT, D_MODEL), lambda t, e: (t // per_b, t % per_b, 0)),
        out_shape=jax.ShapeDtypeStruct((bsz, n_lat, D_MODEL), F32),
        scratch_shapes=[pltpu.VMEM((D_MODEL, PEER_TT), F32),
                        pltpu.VMEM((PEER_ET, PEER_TT), F32),
                        pltpu.VMEM((PEER_ET, PEER_TT), BF16)],
        compiler_params=_cparams(("arbitrary", "arbitrary")),
        name="peer",
    )(h2t_bf, r2, b_fac, ci, a_fac, u_bf, vt_bf, x1, g2, ln2_g.reshape(1, -1), ln2_b.reshape(1, -1))


def kernel(x, c, ctx, c_ctx, w_mod, b_mod, w_in, hg_lb_logits, hg_norm_g, ml_conv_w, ml_conv_b, ml_gate_b,
           ml_norm_g, w_out, ln1_g, ln1_b, peer_wq, peer_keys, peer_u, peer_v, ln2_g, ln2_b):
    bsz, n_lat, _ = x.shape
    n_ctx = ctx.shape[1]
    assert n_ctx % ROW_TILE == 0 and n_lat % ROW_TILE == 0 and ROW_TILE % CHUNK == 0
    assert w_mod.shape[0] == 1, "single-layer kernel"
    lyr = 0

    n_rows = -(-(bsz + 1) // 8) * 8
    c_all = jnp.zeros((n_rows, D_MODEL), F32).at[:bsz].set(c).at[bsz].set(c_ctx)
    mod = _modulation(c_all, w_mod[lyr], b_mod[lyr])
    mod_l, mod_c = mod[:bsz], mod[bsz]
    n_tiles_ctx = n_ctx // ROW_TILE
    n_tiles = (n_ctx + n_lat) // ROW_TILE
    mod1_l = mod_l[:, None, :2 * D_MODEL]
    mod1_c = jnp.broadcast_to(mod_c[None, None, :2 * D_MODEL], (bsz, 1, 2 * D_MODEL))
    mod1 = jnp.concatenate([jnp.repeat(mod1_c, n_tiles_ctx, axis=1),
                            jnp.repeat(mod1_l, n_tiles - n_tiles_ctx, axis=1)], axis=1)[:, :, None, :]
    mod2 = mod_l[:, None, 2 * D_MODEL:5 * D_MODEL]
    g2 = mod_l[:, None, 5 * D_MODEL:6 * D_MODEL]

    x_all = jnp.concatenate([ctx, x], axis=1)
    w_main = w_in[lyr][:, :D_MAIN].astype(BF16)
    w_gates = w_in[lyr][:, D_MAIN:]
    p, gates, gates_t = _input_projection(x_all, mod1, w_main, w_gates, ml_gate_b[lyr])
    n_ch = (n_ctx + n_lat) // CHUNK
    gates_rows = gates_t.reshape(bsz, N_GATES, n_ch, CHUNK).transpose(0, 2, 1, 3)

    qk = _qk_conv(p, ml_conv_w[lyr], ml_conv_b[lyr], n_ctx)
    o_f, o_b = _token_mixers(p, qk, gates, gates_rows, hg_lb_logits[:, lyr:lyr + 2], n_ctx)
    x1, h2t, h2t_bf = _readout_project(o_f, o_b, p, x, mod2, w_out[lyr], hg_norm_g[lyr], ml_norm_g[lyr],
                                       ln1_g[lyr], ln1_b[lyr], n_ctx)
    r2, ci, a_fac, b_fac = _peer_select(h2t, peer_wq[lyr], peer_keys[lyr])
    u_bf = peer_u[lyr].astype(BF16)
    vt_bf = peer_v[lyr].T.astype(BF16)
    return _peer_dense(h2t_bf, r2, ci, a_fac, b_fac, u_bf, vt_bf, x1, g2, ln2_g[lyr], ln2_b[lyr])
```

```python
import functools
import math

import jax
import jax.numpy as jnp
from jax import lax
from jax.experimental import pallas as pl
from jax.experimental.pallas import tpu as pltpu

F32 = jnp.float32
BF16 = jnp.bfloat16

D_MODEL = 1024
CHUNK = 64
GRID_W = 64
N_HEADS = 4
D_HEAD = 128
MIX_W = N_HEADS * D_HEAD
N_GATES = 4 * N_HEADS
D_MAIN = 9 * MIX_W
PEER_HEADS = 8
PEER_NKEYS = 128
PEER_TOPK = 16
PEER_DQ = 256
LN_EPS = 1e-6
DEEPNORM_ALPHA = 2.0 ** 0.25
LANES = 128
VMEM_LIMIT = 56 * 1024 * 1024

COL_HQ, COL_HI, COL_HG, COL_HF0, COL_HF1, COL_MQ, COL_MK, COL_MV, COL_MO = range(9)

ROW_TILE = 256
SEL_TILE = 256
PEER_TT = 512
PEER_ET = 2048

NT_DIMS = (((1,), (1,)), ((), ()))
TN_DIMS = (((0,), (0,)), ((), ()))


def _cparams(sem):
    return pltpu.CompilerParams(dimension_semantics=sem, vmem_limit_bytes=VMEM_LIMIT)


def _split(a, n):
    parts = []
    r = a
    for k in range(n):
        p = r.astype(BF16)
        parts.append(p)
        if k + 1 < n:
            r = r - p.astype(F32)
    return parts


def _mm(a, b, dims=None):
    if dims is None:
        return jnp.dot(a, b, preferred_element_type=F32)
    return lax.dot_general(a, b, dims, preferred_element_type=F32)


def _mm_bf16(a, b, dims=None):
    return _mm(a.astype(BF16), b.astype(BF16), dims)


def _mm_x3(a, b, dims=None):
    a_hi, a_lo = _split(a, 2)
    b_hi, b_lo = _split(b, 2)
    return _mm(a_hi, b_hi, dims) + (_mm(a_hi, b_lo, dims) + _mm(a_lo, b_hi, dims))


def _mm_x3_pre(a, b_hi, b_lo, dims=None):
    a_hi, a_lo = _split(a, 2)
    return _mm(a_hi, b_hi, dims) + (_mm(a_hi, b_lo, dims) + _mm(a_lo, b_hi, dims))


def _mm_exact_left(m_bf16, x):
    x0, x1, x2 = _split(x, 3)
    return _mm(m_bf16, x0) + (_mm(m_bf16, x1) + _mm(m_bf16, x2))


def _mm_exact_right(x, m_bf16):
    x0, x1, x2 = _split(x, 3)
    return _mm(x0, m_bf16) + (_mm(x1, m_bf16) + _mm(x2, m_bf16))


def _sigmoid(x):
    return 1.0 / (1.0 + jnp.exp(-x))


def _silu(x):
    return x * _sigmoid(x)


def _log_sigmoid(x):
    return jnp.minimum(x, 0.0) - jnp.log(1.0 + jnp.exp(-jnp.abs(x)))


def _ln(x):
    mu = jnp.mean(x, axis=-1, keepdims=True)
    xc = x - mu
    var = jnp.mean(xc * xc, axis=-1, keepdims=True)
    return xc * lax.rsqrt(var + LN_EPS)


def _gelu(x):
    return 0.5 * x * (1.0 + lax.erf(x * (2.0 ** -0.5)))


def _mod_kernel(c_ref, w_ref, b_ref, o_ref):
    s = _silu(c_ref[...])
    o_ref[...] = _mm_x3(s, w_ref[...]) + b_ref[...]


def _modulation(c_all, w_mod, b_mod):
    n = c_all.shape[0]
    d_out = w_mod.shape[1]
    blk = 1024
    return pl.pallas_call(
        _mod_kernel,
        grid=(d_out // blk,),
        in_specs=[pl.BlockSpec((n, D_MODEL), lambda j: (0, 0)),
                  pl.BlockSpec((D_MODEL, blk), lambda j: (0, j)),
                  pl.BlockSpec((1, blk), lambda j: (0, j))],
        out_specs=pl.BlockSpec((n, blk), lambda j: (0, j)),
        out_shape=jax.ShapeDtypeStruct((n, d_out), F32),
        compiler_params=_cparams(("arbitrary",)),
        name="mod",
    )(c_all, w_mod, b_mod.reshape(1, d_out))


def _inproj_kernel(x_ref, mod_ref, w_ref, wg_hi_ref, wg_lo_ref, wgt_hi_ref, wgt_lo_ref, gb_ref, gbt_ref,
                   p_ref, g_ref, gt_ref):
    x = x_ref[0]
    shift = mod_ref[0, 0, :, :D_MODEL]
    scale = mod_ref[0, 0, :, D_MODEL:]
    h = _ln(x) * (1.0 + scale) + shift
    p_ref[0] = _mm(h.astype(BF16), w_ref[...])
    h_hi, h_lo = _split(h, 2)
    g = _mm(h_hi, wg_hi_ref[...]) + (_mm(h_hi, wg_lo_ref[...]) + _mm(h_lo, wg_hi_ref[...]))
    g_ref[0] = g + gb_ref[...]
    gt = (_mm(wgt_hi_ref[...], h_hi, NT_DIMS)
          + (_mm(wgt_lo_ref[...], h_hi, NT_DIMS) + _mm(wgt_hi_ref[...], h_lo, NT_DIMS)))
    gt_ref[0] = gt + gbt_ref[...]


def _input_projection(x_all, mod1, w_main, wg, gate_b):
    bsz, t_all, _ = x_all.shape
    n_tiles = t_all // ROW_TILE
    wg_pad = jnp.zeros((D_MODEL, LANES), F32).at[:, :N_GATES].set(wg)
    wg_hi = wg_pad.astype(BF16)
    wg_lo = (wg_pad - wg_hi.astype(F32)).astype(BF16)
    wgt = wg.T
    wgt_hi = wgt.astype(BF16)
    wgt_lo = (wgt - wgt_hi.astype(F32)).astype(BF16)
    gb = jnp.zeros((1, LANES), F32).at[0, :N_GATES].set(gate_b)
    gbt = gate_b.reshape(N_GATES, 1)
    const2 = lambda b, t: (0, 0)
    return pl.pallas_call(
        _inproj_kernel,
        grid=(bsz, n_tiles),
        in_specs=[pl.BlockSpec((1, ROW_TILE, D_MODEL), lambda b, t: (b, t, 0)),
                  pl.BlockSpec((1, 1, 1, 2 * D_MODEL), lambda b, t: (b, t, 0, 0)),
                  pl.BlockSpec((D_MODEL, D_MAIN), const2),
                  pl.BlockSpec((D_MODEL, LANES), const2),
                  pl.BlockSpec((D_MODEL, LANES), const2),
                  pl.BlockSpec((N_GATES, D_MODEL), const2),
                  pl.BlockSpec((N_GATES, D_MODEL), const2),
                  pl.BlockSpec((1, LANES), const2),
                  pl.BlockSpec((N_GATES, 1), const2)],
        out_specs=[pl.BlockSpec((1, ROW_TILE, D_MAIN), lambda b, t: (b, t, 0)),
                   pl.BlockSpec((1, ROW_TILE, LANES), lambda b, t: (b, t, 0)),
                   pl.BlockSpec((1, N_GATES, ROW_TILE), lambda b, t: (b, 0, t))],
        out_shape=[jax.ShapeDtypeStruct((bsz, t_all, D_MAIN), F32),
                   jax.ShapeDtypeStruct((bsz, t_all, LANES), F32),
                   jax.ShapeDtypeStruct((bsz, N_GATES, t_all), F32)],
        compiler_params=_cparams(("arbitrary", "arbitrary")),
        name="inproj",
    )(x_all, mod1, w_main, wg_hi, wg_lo, wgt_hi, wgt_lo, gb, gbt)


CONV_PAD = 72
CONV_CB = 256


def _conv_kernel(n_ctx, n_lat, p_ref, w_ref, b_ref, o_ref, pad_c, pad_l):
    cb = pl.program_id(1)
    w = w_ref[...]
    bias = b_ref[...]
    scale = jnp.where(cb >= MIX_W // CONV_CB, D_HEAD ** -0.5, 1.0).astype(F32)

    def finish(y):
        return _silu(y + bias) * scale

    pad_c[...] = jnp.zeros_like(pad_c)
    pad_c[pl.ds(CONV_PAD, n_ctx), :] = p_ref[0, pl.ds(0, n_ctx), :]
    y = jnp.zeros((n_ctx, CONV_CB), F32)
    for dc in range(3):
        y = y + pad_c[pl.ds(CONV_PAD + dc - 1, n_ctx), :] * w[3 + dc:4 + dc, :]
    o_ref[0, pl.ds(0, n_ctx), :] = finish(y)

    pad_l[...] = jnp.zeros_like(pad_l)
    pad_l[pl.ds(CONV_PAD, n_lat), :] = p_ref[0, pl.ds(n_ctx, n_lat), :]
    col = lax.broadcasted_iota(jnp.int32, (n_lat, CONV_CB), 0) & (GRID_W - 1)
    y = jnp.zeros((n_lat, CONV_CB), F32)
    for dr in range(3):
        for dc in range(3):
            shift = (dr - 1) * GRID_W + (dc - 1)
            t = pad_l[pl.ds(CONV_PAD + shift, n_lat), :] * w[3 * dr + dc:3 * dr + dc + 1, :]
            if dc == 0:
                t = jnp.where(col >= 1, t, 0.0)
            elif dc == 2:
                t = jnp.where(col <= GRID_W - 2, t, 0.0)
            y = y + t
    o_ref[0, pl.ds(n_ctx, n_lat), :] = finish(y)


def _qk_conv(p, conv_w, conv_b, n_ctx):
    bsz, t_all, _ = p.shape
    n_lat = t_all - n_ctx
    n_cb = 2 * MIX_W // CONV_CB
    first = COL_MQ * MIX_W // CONV_CB
    w9 = conv_w.reshape(9, 2 * MIX_W)
    return pl.pallas_call(
        functools.partial(_conv_kernel, n_ctx, n_lat),
        grid=(bsz, n_cb),
        in_specs=[pl.BlockSpec((1, t_all, CONV_CB), lambda b, c: (b, 0, first + c)),
                  pl.BlockSpec((9, CONV_CB), lambda b, c: (0, c)),
                  pl.BlockSpec((1, CONV_CB), lambda b, c: (0, c))],
        out_specs=pl.BlockSpec((1, t_all, CONV_CB), lambda b, c: (b, 0, c)),
        out_shape=jax.ShapeDtypeStruct((bsz, t_all, 2 * MIX_W), F32),
        scratch_shapes=[pltpu.VMEM((n_ctx + 2 * CONV_PAD, CONV_CB), F32),
                        pltpu.VMEM((n_lat + 2 * CONV_PAD, CONV_CB), F32)],
        compiler_params=_cparams(("arbitrary", "arbitrary")),
        name="qkconv",
    )(p, w9, conv_b.reshape(1, 2 * MIX_W))


MIX_BLOCK = 256


def _block_mask(t, rev):
    r = lax.broadcasted_iota(jnp.int32, (t, t), 0)
    c = lax.broadcasted_iota(jnp.int32, (t, t), 1)
    shift = CHUNK.bit_length() - 1
    same = (r >> shift) == (c >> shift)
    return jnp.logical_and(same, (c >= r) if rev else (c <= r))


def _chunk_rows(x, idx):
    return jnp.concatenate(
        [jnp.broadcast_to(x[j * CHUNK + idx:j * CHUNK + idx + 1], (CHUNK, x.shape[1]))
         for j in range(x.shape[0] // CHUNK)], axis=0)


def _scan_order(n, rev):
    return range(n - 1, -1, -1) if rev else range(n)


def _hgrn_dir(d, rev, q_ref, v_ref, z_ref, lb, st_ref, o_ref):
    q = q_ref[0]
    vb = v_ref[0].astype(BF16)
    sig = _sigmoid(z_ref[0])
    log_f = jnp.log(lb + (1.0 - lb) * sig)
    kk = (1.0 - lb) * (1.0 - sig)
    t = q.shape[0]
    n_sub = t // CHUNK
    mask = _block_mask(t, rev)
    tri = jnp.where(mask, 1.0, 0.0).astype(BF16)
    b = _mm_exact_left(tri, log_f)
    last = 0 if rev else CHUNK - 1
    mid = CHUNK // 2 - 1 if rev else CHUNK // 2
    b_last = _chunk_rows(b, last)
    b_mid = _chunk_rows(b, mid)
    qd = (q * jnp.exp(b - b_mid)).astype(BF16)
    kd = (kk * jnp.exp(b_mid - b)).astype(BF16)
    qe = (q * jnp.exp(b)).astype(BF16)
    kdec = (kk * jnp.exp(b_last - b)).astype(BF16)
    for h in range(N_HEADS):
        sl = slice(h * D_HEAD, (h + 1) * D_HEAD)
        sc = _mm(qd[:, sl], kd[:, sl], NT_DIMS)
        o = _mm(jnp.where(mask, sc, 0.0).astype(BF16), vb[:, sl])
        st = st_ref[d, h]
        inter = [None] * n_sub
        for j in _scan_order(n_sub, rev):
            rows = slice(j * CHUNK, (j + 1) * CHUNK)
            inter[j] = _mm(qe[rows, sl], st.astype(BF16), NT_DIMS)
            e_last = jnp.exp(b[j * CHUNK + last:j * CHUNK + last + 1, sl])
            st = st * e_last + _mm(vb[rows, sl], kdec[rows, sl], TN_DIMS)
        st_ref[d, h] = st
        o_ref[0, :, sl] = o + jnp.concatenate(inter, axis=0)


def _mlstm_dir(d, rev, qk_ref, v_ref, gc_ref, gr_ref, c_ref, n_ref, m_ref, o_ref):
    qk = qk_ref[0]
    vb = v_ref[0].astype(BF16)
    gc = gc_ref[0]
    gr = gr_ref[0]
    t = gc.shape[0]
    n_sub = t // CHUNK
    mask = _block_mask(t, rev)
    tri = jnp.where(mask, 1.0, 0.0).astype(BF16)
    tri_t = jnp.where(_block_mask(t, not rev), 1.0, 0.0).astype(BF16)
    a_cols = _mm_exact_left(tri, _log_sigmoid(gc))
    a_rows = _mm_exact_right(_log_sigmoid(gr), tri_t)
    last = 0 if rev else CHUNK - 1
    order = _scan_order(n_sub, rev)

    log_i = pltpu.roll(gc, 2 * N_HEADS, axis=1)
    g = _chunk_rows(a_cols, last) - a_cols + log_i
    m_loc = [jnp.max(g[j * CHUNK:(j + 1) * CHUNK], axis=0, keepdims=True) for j in range(n_sub)]
    w_in = jnp.exp(g - jnp.concatenate([jnp.broadcast_to(r, (CHUNK, LANES)) for r in m_loc], axis=0))
    m = m_ref[d]
    m_before, s_old, s_new = [None] * n_sub, [None] * n_sub, [None] * n_sub
    for j in order:
        a_end = a_cols[j * CHUNK + last:j * CHUNK + last + 1]
        m_before[j] = m
        m_new = jnp.maximum(a_end + m, m_loc[j])
        s_old[j] = jnp.exp(a_end + m - m_new)
        s_new[j] = jnp.exp(m_loc[j] - m_new)
        m = m_new
    m_ref[d] = m
    m_prev_all = jnp.concatenate([jnp.broadcast_to(r, (CHUNK, LANES)) for r in m_before], axis=0)

    for h in range(N_HEADS):
        ic = d * N_HEADS + h
        fc = 2 * N_HEADS + d * N_HEADS + h
        q_h = qk[:, h * D_HEAD:(h + 1) * D_HEAD]
        k_h = qk[:, MIX_W + h * D_HEAD:MIX_W + (h + 1) * D_HEAD]
        v_h = vb[:, h * D_HEAD:(h + 1) * D_HEAD]
        qb = q_h.astype(BF16)
        wk = w_in[:, fc:fc + 1] * k_h
        wkb = wk.astype(BF16)
        c_st = c_ref[d, h]
        n_st = n_ref[d, h]
        inter, qn = [None] * n_sub, [None] * n_sub
        for j in order:
            rows = slice(j * CHUNK, (j + 1) * CHUNK)
            inter[j] = _mm(qb[rows], c_st.astype(BF16))
            qn[j] = jnp.sum(q_h[rows] * n_st, axis=1, keepdims=True)
            so = s_old[j][:, fc:fc + 1]
            sn = s_new[j][:, fc:fc + 1]
            c_st = so * c_st + sn * _mm(wkb[rows], v_h[rows], TN_DIMS)
            n_st = so * n_st + sn * jnp.sum(wk[rows], axis=0, keepdims=True)
        c_ref[d, h] = c_st
        n_ref[d, h] = n_st

        r_row = gr[ic:ic + 1, :] - a_rows[fc:fc + 1, :]
        m_prev = m_prev_all[:, fc:fc + 1]
        r_max = jnp.max(jnp.where(mask, r_row, -jnp.inf), axis=1, keepdims=True)
        mm = jnp.maximum(r_max, m_prev)
        w_intra = _mm(qb, k_h.astype(BF16), NT_DIMS) * jnp.exp(jnp.where(mask, r_row - mm, -jnp.inf))
        w_inter = jnp.exp(m_prev - mm)
        num = _mm(w_intra.astype(BF16), v_h) + w_inter * jnp.concatenate(inter, axis=0)
        den = jnp.sum(w_intra, axis=1, keepdims=True) + w_inter * jnp.concatenate(qn, axis=0)
        floor = jnp.exp(-(a_cols[:, fc:fc + 1] + mm))
        o_ref[0, :, MIX_W + h * D_HEAD:MIX_W + (h + 1) * D_HEAD] = num / jnp.maximum(jnp.abs(den), floor)


def _mix_kernel(lbl_ref,
                hq_f, hi_f, hf_f, mv_f, qk_f, gc_f, gr_f,
                hq_b, hi_b, hf_b, mv_b, qk_b, gc_b, gr_b,
                of_ref, ob_ref, st_ref, c_ref, n_ref, m_ref):
    @pl.when(pl.program_id(1) == 0)
    def _():
        st_ref[...] = jnp.zeros_like(st_ref)
        c_ref[...] = jnp.zeros_like(c_ref)
        n_ref[...] = jnp.zeros_like(n_ref)
        m_ref[...] = jnp.zeros_like(m_ref)

    for d, rev, (hq, hi, hf, mv, qk, gc, gr, o_ref) in (
            (0, False, (hq_f, hi_f, hf_f, mv_f, qk_f, gc_f, gr_f, of_ref)),
            (1, True, (hq_b, hi_b, hf_b, mv_b, qk_b, gc_b, gr_b, ob_ref))):
        logits = lbl_ref[d]
        mx = jnp.max(logits, axis=0, keepdims=True)
        ex = jnp.exp(logits - mx)
        lb = ex[0:1, :] / jnp.sum(ex, axis=0, keepdims=True)
        _hgrn_dir(d, rev, hq, hi, hf, lb, st_ref, o_ref)
        _mlstm_dir(d, rev, qk, mv, gc, gr, c_ref, n_ref, m_ref, o_ref)


def _token_mixers(p, qk, gates, gates_t, lb_logits, n_ctx):
    bsz, t_all, _ = p.shape
    n_lat = t_all - n_ctx
    n_cb = n_ctx // MIX_BLOCK
    n_blk = t_all // MIX_BLOCK

    def f_idx(s):
        return s

    def b_idx(s):
        return jnp.where(s < n_cb, n_cb - 1 - s, n_blk - 1 + n_cb - s)

    def pspec(col, idx):
        return pl.BlockSpec((1, MIX_BLOCK, MIX_W), lambda b, s: (b, idx(s), col))

    def dir_specs(idx, col_f):
        return [pspec(COL_HQ, idx), pspec(COL_HI, idx), pspec(col_f, idx), pspec(COL_MV, idx),
                pl.BlockSpec((1, MIX_BLOCK, 2 * MIX_W), lambda b, s: (b, idx(s), 0)),
                pl.BlockSpec((1, MIX_BLOCK, LANES), lambda b, s: (b, idx(s), 0)),
                pl.BlockSpec((1, N_GATES, MIX_BLOCK), lambda b, s: (b, 0, idx(s)))]

    out_f = pl.BlockSpec((1, MIX_BLOCK, 2 * MIX_W), lambda b, s: (b, jnp.maximum(s - n_cb, 0), 0))
    out_b = pl.BlockSpec((1, MIX_BLOCK, 2 * MIX_W),
                         lambda b, s: (b, jnp.minimum(n_blk - 1 - s, n_blk - 1 - n_cb), 0))
    args_dir = [p, p, p, p, qk, gates, gates_t]
    return pl.pallas_call(
        _mix_kernel,
        grid=(bsz, n_blk),
        in_specs=[pl.BlockSpec((2, 2, MIX_W), lambda b, s: (0, 0, 0))]
        + dir_specs(f_idx, COL_HF0) + dir_specs(b_idx, COL_HF1),
        out_specs=[out_f, out_b],
        out_shape=[jax.ShapeDtypeStruct((bsz, n_lat, 2 * MIX_W), F32)] * 2,
        scratch_shapes=[pltpu.VMEM((2, N_HEADS, D_HEAD, D_HEAD), F32),
                        pltpu.VMEM((2, N_HEADS, D_HEAD, D_HEAD), F32),
                        pltpu.VMEM((2, N_HEADS, 1, D_HEAD), F32),
                        pltpu.VMEM((2, 1, LANES), F32)],
        compiler_params=_cparams(("arbitrary", "arbitrary")),
        name="mix",
    )(lb_logits, *args_dir, *args_dir)


def _out_kernel(of_ref, ob_ref, hg_ref, mo_ref, x_ref, mod_ref, w_ref, hgn_ref, mln_ref, lng_ref, lnb_ref,
                x1_ref, h2t_ref, h2t_bf_ref):
    o = of_ref[0] + ob_ref[0]
    hg = hg_ref[0]
    mo = mo_ref[0]
    parts = []
    for h in range(N_HEADS):
        sl = slice(h * D_HEAD, (h + 1) * D_HEAD)
        oh = o[:, sl]
        y = oh * lax.rsqrt(jnp.mean(oh * oh, axis=-1, keepdims=True) + LN_EPS) * hgn_ref[:, sl]
        parts.append(y * _silu(hg[:, sl]))
    for h in range(N_HEADS):
        sl = slice(h * D_HEAD, (h + 1) * D_HEAD)
        oh = o[:, MIX_W + h * D_HEAD:MIX_W + (h + 1) * D_HEAD]
        parts.append(_ln(oh) * mln_ref[:, sl] * _sigmoid(mo[:, sl]))
    mix = jnp.concatenate(parts, axis=-1).astype(BF16)
    y = _mm(mix, w_ref[...])
    g1 = mod_ref[0, :, 0:D_MODEL]
    sh2 = mod_ref[0, :, D_MODEL:2 * D_MODEL]
    sc2 = mod_ref[0, :, 2 * D_MODEL:3 * D_MODEL]
    x1 = _ln(DEEPNORM_ALPHA * x_ref[0] + g1 * y) * lng_ref[...] + lnb_ref[...]
    x1_ref[0] = x1
    h2t = (_ln(x1) * (1.0 + sc2) + sh2).T
    h2t_ref[...] = h2t
    h2t_bf_ref[...] = h2t.astype(BF16)


def _readout_project(o_f, o_b, p, x, mod2, w_out, hg_norm_g, ml_norm_g, ln1_g, ln1_b, n_ctx):
    bsz, n_lat, _ = x.shape
    n_tiles = n_lat // ROW_TILE
    off = n_ctx // ROW_TILE
    row = lambda a: a.reshape(1, -1)
    const2 = lambda b, t: (0, 0)
    return pl.pallas_call(
        _out_kernel,
        grid=(bsz, n_tiles),
        in_specs=[pl.BlockSpec((1, ROW_TILE, 2 * MIX_W), lambda b, t: (b, t, 0)),
                  pl.BlockSpec((1, ROW_TILE, 2 * MIX_W), lambda b, t: (b, t, 0)),
                  pl.BlockSpec((1, ROW_TILE, MIX_W), lambda b, t: (b, t + off, COL_HG)),
                  pl.BlockSpec((1, ROW_TILE, MIX_W), lambda b, t: (b, t + off, COL_MO)),
                  pl.BlockSpec((1, ROW_TILE, D_MODEL), lambda b, t: (b, t, 0)),
                  pl.BlockSpec((1, 1, 3 * D_MODEL), lambda b, t: (b, 0, 0)),
                  pl.BlockSpec((2 * MIX_W, D_MODEL), const2),
                  pl.BlockSpec((1, MIX_W), const2),
                  pl.BlockSpec((1, MIX_W), const2),
                  pl.BlockSpec((1, D_MODEL), const2),
                  pl.BlockSpec((1, D_MODEL), const2)],
        out_specs=[pl.BlockSpec((1, ROW_TILE, D_MODEL), lambda b, t: (b, t, 0)),
                   pl.BlockSpec((D_MODEL, ROW_TILE), lambda b, t: (0, b * n_tiles + t)),
                   pl.BlockSpec((D_MODEL, ROW_TILE), lambda b, t: (0, b * n_tiles + t))],
        out_shape=[jax.ShapeDtypeStruct((bsz, n_lat, D_MODEL), F32),
                   jax.ShapeDtypeStruct((D_MODEL, bsz * n_lat), F32),
                   jax.ShapeDtypeStruct((D_MODEL, bsz * n_lat), BF16)],
        compiler_params=_cparams(("arbitrary", "arbitrary")),
        name="readout",
    )(o_f, o_b, p, p, x, mod2, w_out.astype(BF16), row(hg_norm_g), row(ml_norm_g), row(ln1_g), row(ln1_b))


def _extract_exact(s):
    ridx = lax.broadcasted_iota(jnp.int32, s.shape, 0).astype(F32)
    rank = jnp.full(s.shape, float(PEER_TOPK), F32)
    vals = []
    for it in range(PEER_TOPK):
        m = jnp.max(s, axis=0, keepdims=True)
        first = jnp.min(jnp.where(s == m, ridx, float(s.shape[0])), axis=0, keepdims=True)
        hit = ridx == first
        rank = jnp.where(hit, float(it), rank)
        s = jnp.where(hit, -jnp.inf, s)
        vals.append(m)
    return rank, vals


def _extract_fast(s):
    rank = jnp.full(s.shape, float(PEER_TOPK), F32)
    vals = []
    for it in range(PEER_TOPK):
        m = jnp.max(s, axis=0, keepdims=True)
        hit = s == m
        rank = jnp.where(hit, float(it), rank)
        s = jnp.where(hit, -jnp.inf, s)
        vals.append(m)
    return rank, vals


def _count_true(mask):
    return jnp.sum(jnp.where(mask, 1.0, 0.0), axis=0, keepdims=True)


def _cand_grid(v1, v2):
    v1a = jnp.concatenate(v1, axis=0)
    v2a = jnp.concatenate(v2, axis=0)
    row = lax.broadcasted_iota(jnp.int32, (8, v1a.shape[1]), 0)
    blocks = [v1a[0:1] + v2a]
    for a in range(1, 8):
        nb = PEER_TOPK // (a + 1)
        blk = v1a[a:a + 1] + v2a[0:8]
        blocks.append(blk if nb >= 8 else jnp.where(row < nb, blk, -jnp.inf))
    blocks.append(v1a[8:16] + v2a[0:1])
    return jnp.concatenate(blocks, axis=0)


def _cand_counts(chosen):
    cnt = [_count_true(chosen[0:PEER_TOPK])]
    for a in range(1, 8):
        cnt.append(_count_true(chosen[PEER_TOPK + 8 * (a - 1):PEER_TOPK + 8 * a]))
    base = PEER_TOPK + 8 * 7
    for r in range(8):
        cnt.append(jnp.where(chosen[base + r:base + r + 1], 1.0, 0.0))
    return cnt


def _sel_emit(hd, s1, s2, rank1, rank2, v1, v2, cand, chosen, r2_ref, ci_ref, a_ref, b_ref):
    top = v1[0] + v2[0]
    z = jnp.sum(jnp.where(chosen, jnp.exp(cand - top), 0.0), axis=0, keepdims=True)
    ci = jnp.zeros_like(rank1)
    for a, cnt_a in enumerate(_cand_counts(chosen)):
        ci = jnp.where(rank1 == float(a), cnt_a, ci)
    r2_ref[hd] = rank2.astype(BF16)
    ci_ref[hd] = ci
    a_ref[hd] = jnp.exp(s1 - v1[0])
    b_ref[hd] = (jnp.exp(s2 - v2[0]) / z).astype(BF16)


def _sel_kernel(h_ref, wq_hi_ref, wq_lo_ref, k_hi_ref, k_lo_ref, r2_ref, ci_ref, a_ref, b_ref):
    h2t = h_ref[...]
    h_hi, h_lo = _split(h2t, 2)
    qt = (_mm(wq_hi_ref[...], h_hi) + (_mm(wq_lo_ref[...], h_hi) + _mm(wq_hi_ref[...], h_lo)))
    half = PEER_DQ // 2
    for hd in range(PEER_HEADS):
        s = []
        for p_ in range(2):
            qh = qt[hd * PEER_DQ + p_ * half: hd * PEER_DQ + (p_ + 1) * half, :]
            q_hi, q_lo = _split(qh, 2)
            k_hi = k_hi_ref[hd, p_]
            k_lo = k_lo_ref[hd, p_]
            s.append(_mm(k_hi, q_hi) + (_mm(k_lo, q_hi) + _mm(k_hi, q_lo)))
        s1, s2 = s

        rank1, v1 = _extract_fast(s1)
        rank2, v2 = _extract_fast(s2)
        cand = _cand_grid(v1, v2)
        _, cv = _extract_fast(cand)
        chosen = cand >= cv[PEER_TOPK - 1]
        k = float(PEER_TOPK)
        tied = ((_count_true(rank1 < k) != k) | (_count_true(rank2 < k) != k) | (_count_true(chosen) != k))
        any_tied = jnp.max(jnp.where(tied, 1.0, 0.0)) > 0.0
        _sel_emit(hd, s1, s2, rank1, rank2, v1, v2, cand, chosen, r2_ref, ci_ref, a_ref, b_ref)

        @pl.when(any_tied)
        def _():
            rank1, v1 = _extract_exact(s1)
            rank2, v2 = _extract_exact(s2)
            cand = _cand_grid(v1, v2)
            crank, _ = _extract_exact(cand)
            _sel_emit(hd, s1, s2, rank1, rank2, v1, v2, cand, crank < k, r2_ref, ci_ref, a_ref, b_ref)


def _peer_select(h2t, peer_wq, peer_keys):
    n_tok = h2t.shape[1]
    wqt = peer_wq.T
    wq_hi = wqt.astype(BF16)
    wq_lo = (wqt - wq_hi.astype(F32)).astype(BF16)
    k_hi = peer_keys.astype(BF16)
    k_lo = (peer_keys - k_hi.astype(F32)).astype(BF16)
    dq_all = PEER_HEADS * PEER_DQ
    out_spec = pl.BlockSpec((PEER_HEADS, PEER_NKEYS, SEL_TILE), lambda t: (0, 0, t))
    shape = (PEER_HEADS, PEER_NKEYS, n_tok)
    return pl.pallas_call(
        _sel_kernel,
        grid=(n_tok // SEL_TILE,),
        in_specs=[pl.BlockSpec((D_MODEL, SEL_TILE), lambda t: (0, t)),
                  pl.BlockSpec((dq_all, D_MODEL), lambda t: (0, 0)),
                  pl.BlockSpec((dq_all, D_MODEL), lambda t: (0, 0)),
                  pl.BlockSpec((PEER_HEADS, 2, PEER_NKEYS, PEER_DQ // 2), lambda t: (0, 0, 0, 0)),
                  pl.BlockSpec((PEER_HEADS, 2, PEER_NKEYS, PEER_DQ // 2), lambda t: (0, 0, 0, 0))],
        out_specs=[out_spec] * 4,
        out_shape=[jax.ShapeDtypeStruct(shape, BF16), jax.ShapeDtypeStruct(shape, F32),
                   jax.ShapeDtypeStruct(shape, F32), jax.ShapeDtypeStruct(shape, BF16)],
        compiler_params=_cparams(("arbitrary",)),
        name="peersel",
    )(h2t, wq_hi, wq_lo, k_hi, k_lo)


def _rows_bf16(row):
    tile = jnp.broadcast_to(row, (16, row.shape[1])).astype(BF16)
    return jnp.concatenate([tile] * (PEER_NKEYS // 16), axis=0)


def _peer_kernel(n_e, h_ref, r2_ref, b_ref, ci_ref, a_ref, u_ref, vt_ref, x1_ref, g2_ref, lng_ref, lnb_ref,
                 o_ref, acc_ref, pre_ref, wa_ref):
    e = pl.program_id(1)

    @pl.when(e == 0)
    def _():
        acc_ref[...] = jnp.zeros_like(acc_ref)

    pre_ref[...] = _mm(u_ref[...], h_ref[...])
    zero = jnp.zeros((), BF16)
    for ii in range(PEER_ET // PEER_NKEYS):
        w = None
        for hd in range(PEER_HEADS):
            cnt = _rows_bf16(ci_ref[hd, ii:ii + 1, :])
            fac = _rows_bf16(a_ref[hd, ii:ii + 1, :])
            term = jnp.where(r2_ref[hd] < cnt, b_ref[hd] * fac, zero)
            w = term if w is None else w + term
        rows = pl.ds(ii * PEER_NKEYS, PEER_NKEYS)
        wa_ref[rows, :] = w * _gelu(pre_ref[rows, :]).astype(BF16)
    acc_ref[...] += _mm(vt_ref[...], wa_ref[...])

    @pl.when(e == n_e - 1)
    def _():
        y = acc_ref[...].T
        o_ref[0] = _ln(DEEPNORM_ALPHA * x1_ref[0] + g2_ref[0] * y) * lng_ref[...] + lnb_ref[...]


def _peer_dense(h2t_bf, r2, ci, a_fac, b_fac, u_bf, vt_bf, x1, g2, ln2_g, ln2_b):
    bsz, n_lat, _ = x1.shape
    n_tok = bsz * n_lat
    n_t = n_tok // PEER_TT
    per_b = n_lat // PEER_TT
    n_e = u_bf.shape[0] // PEER_ET
    per = PEER_ET // PEER_NKEYS
    assert per % 8 == 0, "whole sublane tiles of half-1 keys per expert step"
    tok_spec = pl.BlockSpec((PEER_HEADS, PEER_NKEYS, PEER_TT), lambda t, e: (0, 0, t))
    key_spec = pl.BlockSpec((PEER_HEADS, per, PEER_TT), lambda t, e: (0, e, t))
    const2 = lambda t, e: (0, 0)
    return pl.pallas_call(
        functools.partial(_peer_kernel, n_e),
        grid=(n_t, n_e),
        in_specs=[pl.BlockSpec((D_MODEL, PEER_TT), lambda t, e: (0, t)),
                  tok_spec, tok_spec, key_spec, key_spec,
                  pl.BlockSpec((PEER_ET, D_MODEL), lambda t, e: (e, 0)),
                  pl.BlockSpec((D_MODEL, PEER_ET), lambda t, e: (0, e)),
                  pl.BlockSpec((1, PEER_TT, D_MODEL), lambda t, e: (t // per_b, t % per_b, 0)),
                  pl.BlockSpec((1, 1, D_MODEL), lambda t, e: (t // per_b, 0, 0)),
                  pl.BlockSpec((1, D_MODEL), const2),
                  pl.BlockSpec((1, D_MODEL), const2)],
        out_specs=pl.BlockSpec((1, PEER_TT, D_MODEL), lambda t, e: (t // per_b, t % per_b, 0)),
        out_shape=jax.ShapeDtypeStruct((bsz, n_lat, D_MODEL), F32),
        scratch_shapes=[pltpu.VMEM((D_MODEL, PEER_TT), F32),
                        pltpu.VMEM((PEER_ET, PEER_TT), F32),
                        pltpu.VMEM((PEER_ET, PEER_TT), BF16)],
        compiler_params=_cparams(("arbitrary", "arbitrary")),
        name="peer",
    )(h2t_bf, r2, b_fac, ci, a_fac, u_bf, vt_bf, x1, g2, ln2_g.reshape(1, -1), ln2_b.reshape(1, -1))


def kernel(x, c, ctx, c_ctx, w_mod, b_mod, w_in, hg_lb_logits, hg_norm_g, ml_conv_w, ml_conv_b, ml_gate_b,
           ml_norm_g, w_out, ln1_g, ln1_b, peer_wq, peer_keys, peer_u, peer_v, ln2_g, ln2_b):
    bsz, n_lat, _ = x.shape
    n_ctx = ctx.shape[1]
    assert n_ctx % ROW_TILE == 0 and n_lat % ROW_TILE == 0 and ROW_TILE % CHUNK == 0
    assert n_ctx % MIX_BLOCK == 0 and n_lat % MIX_BLOCK == 0 and MIX_BLOCK % CHUNK == 0
    assert w_mod.shape[0] == 1, "single-layer kernel"
    lyr = 0

    n_rows = -(-(bsz + 1) // 8) * 8
    c_all = jnp.zeros((n_rows, D_MODEL), F32).at[:bsz].set(c).at[bsz].set(c_ctx)
    mod = _modulation(c_all, w_mod[lyr], b_mod[lyr])
    mod_l, mod_c = mod[:bsz], mod[bsz]
    n_tiles_ctx = n_ctx // ROW_TILE
    n_tiles = (n_ctx + n_lat) // ROW_TILE
    mod1_l = mod_l[:, None, :2 * D_MODEL]
    mod1_c = jnp.broadcast_to(mod_c[None, None, :2 * D_MODEL], (bsz, 1, 2 * D_MODEL))
    mod1 = jnp.concatenate([jnp.repeat(mod1_c, n_tiles_ctx, axis=1),
                            jnp.repeat(mod1_l, n_tiles - n_tiles_ctx, axis=1)], axis=1)[:, :, None, :]
    mod2 = mod_l[:, None, 2 * D_MODEL:5 * D_MODEL]
    g2 = mod_l[:, None, 5 * D_MODEL:6 * D_MODEL]

    x_all = jnp.concatenate([ctx, x], axis=1)
    w_main = w_in[lyr][:, :D_MAIN].astype(BF16)
    w_gates = w_in[lyr][:, D_MAIN:]
    p, gates, gates_t = _input_projection(x_all, mod1, w_main, w_gates, ml_gate_b[lyr])

    qk = _qk_conv(p, ml_conv_w[lyr], ml_conv_b[lyr], n_ctx)
    o_f, o_b = _token_mixers(p, qk, gates, gates_t, hg_lb_logits[:, lyr:lyr + 2], n_ctx)
    x1, h2t, h2t_bf = _readout_project(o_f, o_b, p, x, mod2, w_out[lyr], hg_norm_g[lyr], ml_norm_g[lyr],
                                       ln1_g[lyr], ln1_b[lyr], n_ctx)
    r2, ci, a_fac, b_fac = _peer_select(h2t, peer_wq[lyr], peer_keys[lyr])
    u_bf = peer_u[lyr].astype(BF16)
    vt_bf = peer_v[lyr].T.astype(BF16)
    return _peer_dense(h2t_bf, r2, ci, a_fac, b_fac, u_bf, vt_bf, x1, g2, ln2_g[lyr], ln2_b[lyr])
```

```python
import functools
import math

import jax
import jax.numpy as jnp
from jax import lax
from jax.experimental import pallas as pl
from jax.experimental.pallas import tpu as pltpu

F32 = jnp.float32
BF16 = jnp.bfloat16

D_MODEL = 1024
CHUNK = 64
GRID_W = 64
N_HEADS = 4
D_HEAD = 128
MIX_W = N_HEADS * D_HEAD
N_GATES = 4 * N_HEADS
D_MAIN = 9 * MIX_W
PEER_HEADS = 8
PEER_NKEYS = 128
PEER_TOPK = 16
PEER_DQ = 256
LN_EPS = 1e-6
DEEPNORM_ALPHA = 2.0 ** 0.25
LANES = 128
VMEM_LIMIT = 56 * 1024 * 1024

COL_HQ, COL_HI, COL_HG, COL_HF0, COL_HF1, COL_MQ, COL_MK, COL_MV, COL_MO = range(9)

ROW_TILE = 256
SEL_TILE = 256
PEER_TT = 512
PEER_ET = 2048

NT_DIMS = (((1,), (1,)), ((), ()))
TN_DIMS = (((0,), (0,)), ((), ()))


def _cparams(sem):
    return pltpu.CompilerParams(dimension_semantics=sem, vmem_limit_bytes=VMEM_LIMIT)


def _split(a, n):
    parts = []
    r = a
    for k in range(n):
        p = r.astype(BF16)
        parts.append(p)
        if k + 1 < n:
            r = r - p.astype(F32)
    return parts


def _mm(a, b, dims=None):
    if dims is None:
        return jnp.dot(a, b, preferred_element_type=F32)
    return lax.dot_general(a, b, dims, preferred_element_type=F32)


def _mm_bf16(a, b, dims=None):
    return _mm(a.astype(BF16), b.astype(BF16), dims)


def _mm_x3(a, b, dims=None):
    a_hi, a_lo = _split(a, 2)
    b_hi, b_lo = _split(b, 2)
    return _mm(a_hi, b_hi, dims) + (_mm(a_hi, b_lo, dims) + _mm(a_lo, b_hi, dims))


def _mm_x3_pre(a, b_hi, b_lo, dims=None):
    a_hi, a_lo = _split(a, 2)
    return _mm(a_hi, b_hi, dims) + (_mm(a_hi, b_lo, dims) + _mm(a_lo, b_hi, dims))


def _mm_exact_left(m_bf16, x):
    x0, x1, x2 = _split(x, 3)
    return _mm(m_bf16, x0) + (_mm(m_bf16, x1) + _mm(m_bf16, x2))


def _mm_exact_right(x, m_bf16):
    x0, x1, x2 = _split(x, 3)
    return _mm(x0, m_bf16) + (_mm(x1, m_bf16) + _mm(x2, m_bf16))


def _sigmoid(x):
    return 1.0 / (1.0 + jnp.exp(-x))


def _silu(x):
    return x * _sigmoid(x)


def _log_sigmoid(x):
    return jnp.minimum(x, 0.0) - jnp.log(1.0 + jnp.exp(-jnp.abs(x)))


def _ln(x):
    mu = jnp.mean(x, axis=-1, keepdims=True)
    xc = x - mu
    var = jnp.mean(xc * xc, axis=-1, keepdims=True)
    return xc * lax.rsqrt(var + LN_EPS)


def _gelu(x):
    return 0.5 * x * (1.0 + lax.erf(x * (2.0 ** -0.5)))


def _mod_kernel(c_ref, w_ref, b_ref, o_ref):
    s = _silu(c_ref[...])
    o_ref[...] = _mm_x3(s, w_ref[...]) + b_ref[...]


def _modulation(c_all, w_mod, b_mod):
    n = c_all.shape[0]
    d_out = w_mod.shape[1]
    blk = 1024
    return pl.pallas_call(
        _mod_kernel,
        grid=(d_out // blk,),
        in_specs=[pl.BlockSpec((n, D_MODEL), lambda j: (0, 0)),
                  pl.BlockSpec((D_MODEL, blk), lambda j: (0, j)),
                  pl.BlockSpec((1, blk), lambda j: (0, j))],
        out_specs=pl.BlockSpec((n, blk), lambda j: (0, j)),
        out_shape=jax.ShapeDtypeStruct((n, d_out), F32),
        compiler_params=_cparams(("arbitrary",)),
        name="mod",
    )(c_all, w_mod, b_mod.reshape(1, d_out))


def _inproj_kernel(n_ctx_tiles, ctx_ref, x_ref, mod_ref, w_ref, wg_hi_ref, wg_lo_ref, gb_ref,
                   p_ref, g_ref, gt_ref):
    x = jnp.where(pl.program_id(1) < n_ctx_tiles, ctx_ref[0], x_ref[0])
    shift = mod_ref[0, 0, :, :D_MODEL]
    scale = mod_ref[0, 0, :, D_MODEL:]
    h = _ln(x) * (1.0 + scale) + shift
    p_ref[0] = _mm(h.astype(BF16), w_ref[...])
    h_hi, h_lo = _split(h, 2)
    g = _mm(h_hi, wg_hi_ref[...]) + (_mm(h_hi, wg_lo_ref[...]) + _mm(h_lo, wg_hi_ref[...])) + gb_ref[...]
    g_ref[0] = g
    gt_ref[0] = g.T[:N_GATES, :]


def _input_projection(ctx, x, mod1, w_main, wg, gate_b):
    bsz, n_ctx, _ = ctx.shape
    n_ctx_tiles = n_ctx // ROW_TILE
    t_all = n_ctx + x.shape[1]
    n_tiles = t_all // ROW_TILE
    wg_pad = jnp.zeros((D_MODEL, LANES), F32).at[:, :N_GATES].set(wg)
    wg_hi = wg_pad.astype(BF16)
    wg_lo = (wg_pad - wg_hi.astype(F32)).astype(BF16)
    gb = jnp.zeros((1, LANES), F32).at[0, :N_GATES].set(gate_b)
    const2 = lambda b, t: (0, 0)
    return pl.pallas_call(
        functools.partial(_inproj_kernel, n_ctx_tiles),
        grid=(bsz, n_tiles),
        in_specs=[pl.BlockSpec((1, ROW_TILE, D_MODEL), lambda b, t: (b, jnp.minimum(t, n_ctx_tiles - 1), 0)),
                  pl.BlockSpec((1, ROW_TILE, D_MODEL), lambda b, t: (b, jnp.maximum(t - n_ctx_tiles, 0), 0)),
                  pl.BlockSpec((1, 1, 1, 2 * D_MODEL), lambda b, t: (b, t, 0, 0)),
                  pl.BlockSpec((D_MODEL, D_MAIN), const2),
                  pl.BlockSpec((D_MODEL, LANES), const2),
                  pl.BlockSpec((D_MODEL, LANES), const2),
                  pl.BlockSpec((1, LANES), const2)],
        out_specs=[pl.BlockSpec((1, ROW_TILE, D_MAIN), lambda b, t: (b, t, 0)),
                   pl.BlockSpec((1, ROW_TILE, LANES), lambda b, t: (b, t, 0)),
                   pl.BlockSpec((1, N_GATES, ROW_TILE), lambda b, t: (b, 0, t))],
        out_shape=[jax.ShapeDtypeStruct((bsz, t_all, D_MAIN), F32),
                   jax.ShapeDtypeStruct((bsz, t_all, LANES), F32),
                   jax.ShapeDtypeStruct((bsz, N_GATES, t_all), F32)],
        compiler_params=_cparams(("arbitrary", "arbitrary")),
        name="inproj",
    )(ctx, x, mod1, w_main, wg_hi, wg_lo, gb)


CONV_PAD = 72
CONV_CB = 256


def _conv_kernel(n_ctx, n_lat, p_ref, w_ref, b_ref, o_ref, pad_c, pad_l):
    cb = pl.program_id(1)
    w = w_ref[...]
    bias = b_ref[...]
    scale = jnp.where(cb >= MIX_W // CONV_CB, D_HEAD ** -0.5, 1.0).astype(F32)

    def finish(y):
        return _silu(y + bias) * scale

    pad_c[...] = jnp.zeros_like(pad_c)
    pad_c[pl.ds(CONV_PAD, n_ctx), :] = p_ref[0, pl.ds(0, n_ctx), :]
    y = jnp.zeros((n_ctx, CONV_CB), F32)
    for dc in range(3):
        y = y + pad_c[pl.ds(CONV_PAD + dc - 1, n_ctx), :] * w[3 + dc:4 + dc, :]
    o_ref[0, pl.ds(0, n_ctx), :] = finish(y)

    pad_l[...] = jnp.zeros_like(pad_l)
    pad_l[pl.ds(CONV_PAD, n_lat), :] = p_ref[0, pl.ds(n_ctx, n_lat), :]
    col = lax.broadcasted_iota(jnp.int32, (n_lat, CONV_CB), 0) & (GRID_W - 1)
    y = jnp.zeros((n_lat, CONV_CB), F32)
    for dr in range(3):
        for dc in range(3):
            shift = (dr - 1) * GRID_W + (dc - 1)
            t = pad_l[pl.ds(CONV_PAD + shift, n_lat), :] * w[3 * dr + dc:3 * dr + dc + 1, :]
            if dc == 0:
                t = jnp.where(col >= 1, t, 0.0)
            elif dc == 2:
                t = jnp.where(col <= GRID_W - 2, t, 0.0)
            y = y + t
    o_ref[0, pl.ds(n_ctx, n_lat), :] = finish(y)


def _qk_conv(p, conv_w, conv_b, n_ctx):
    bsz, t_all, _ = p.shape
    n_lat = t_all - n_ctx
    n_cb = 2 * MIX_W // CONV_CB
    first = COL_MQ * MIX_W // CONV_CB
    w9 = conv_w.reshape(9, 2 * MIX_W)
    return pl.pallas_call(
        functools.partial(_conv_kernel, n_ctx, n_lat),
        grid=(bsz, n_cb),
        in_specs=[pl.BlockSpec((1, t_all, CONV_CB), lambda b, c: (b, 0, first + c)),
                  pl.BlockSpec((9, CONV_CB), lambda b, c: (0, c)),
                  pl.BlockSpec((1, CONV_CB), lambda b, c: (0, c))],
        out_specs=pl.BlockSpec((1, t_all, CONV_CB), lambda b, c: (b, 0, c)),
        out_shape=jax.ShapeDtypeStruct((bsz, t_all, 2 * MIX_W), F32),
        scratch_shapes=[pltpu.VMEM((n_ctx + 2 * CONV_PAD, CONV_CB), F32),
                        pltpu.VMEM((n_lat + 2 * CONV_PAD, CONV_CB), F32)],
        compiler_params=_cparams(("arbitrary", "arbitrary")),
        name="qkconv",
    )(p, w9, conv_b.reshape(1, 2 * MIX_W))


MIX_BLOCK = 256


def _block_mask(t, rev):
    r = lax.broadcasted_iota(jnp.int32, (t, t), 0)
    c = lax.broadcasted_iota(jnp.int32, (t, t), 1)
    shift = CHUNK.bit_length() - 1
    same = (r >> shift) == (c >> shift)
    return jnp.logical_and(same, (c >= r) if rev else (c <= r))


def _chunk_rows(x, idx):
    return jnp.concatenate(
        [jnp.broadcast_to(x[j * CHUNK + idx:j * CHUNK + idx + 1], (CHUNK, x.shape[1]))
         for j in range(x.shape[0] // CHUNK)], axis=0)


def _scan_order(n, rev):
    return range(n - 1, -1, -1) if rev else range(n)


def _hgrn_dir(d, rev, want_out, q_ref, v_ref, z_ref, lb, st_ref, o_ref):
    vb = v_ref[0].astype(BF16)
    sig = _sigmoid(z_ref[0])
    log_f = jnp.log(lb + (1.0 - lb) * sig)
    kk = (1.0 - lb) * (1.0 - sig)
    t = vb.shape[0]
    n_sub = t // CHUNK
    mask = _block_mask(t, rev)
    tri = jnp.where(mask, 1.0, 0.0).astype(BF16)
    b = _mm_exact_left(tri, log_f)
    last = 0 if rev else CHUNK - 1
    mid = CHUNK // 2 - 1 if rev else CHUNK // 2
    kdec = (kk * jnp.exp(_chunk_rows(b, last) - b)).astype(BF16)
    if want_out:
        q = q_ref[0]
        b_mid = _chunk_rows(b, mid)
        qd = (q * jnp.exp(b - b_mid)).astype(BF16)
        kd = (kk * jnp.exp(b_mid - b)).astype(BF16)
        qe = (q * jnp.exp(b)).astype(BF16)
    for h in range(N_HEADS):
        sl = slice(h * D_HEAD, (h + 1) * D_HEAD)
        st = st_ref[d, h]
        inter = [None] * n_sub
        for j in _scan_order(n_sub, rev):
            rows = slice(j * CHUNK, (j + 1) * CHUNK)
            if want_out:
                inter[j] = _mm(qe[rows, sl], st.astype(BF16), NT_DIMS)
            e_last = jnp.exp(b[j * CHUNK + last:j * CHUNK + last + 1, sl])
            st = st * e_last + _mm(vb[rows, sl], kdec[rows, sl], TN_DIMS)
        st_ref[d, h] = st
        if want_out:
            sc = _mm(qd[:, sl], kd[:, sl], NT_DIMS)
            o = _mm(jnp.where(mask, sc, 0.0).astype(BF16), vb[:, sl])
            o_ref[0, :, sl] = o + jnp.concatenate(inter, axis=0)


def _mlstm_dir(d, rev, want_out, qk_ref, v_ref, gc_ref, gr_ref, c_ref, n_ref, m_ref, o_ref):
    qk = qk_ref[0]
    vb = v_ref[0].astype(BF16)
    gc = gc_ref[0]
    gr = gr_ref[0]
    t = gc.shape[0]
    n_sub = t // CHUNK
    mask = _block_mask(t, rev)
    tri = jnp.where(mask, 1.0, 0.0).astype(BF16)
    a_cols = _mm_exact_left(tri, _log_sigmoid(gc))
    if want_out:
        tri_t = jnp.where(_block_mask(t, not rev), 1.0, 0.0).astype(BF16)
        a_rows = _mm_exact_right(_log_sigmoid(gr), tri_t)
    last = 0 if rev else CHUNK - 1
    order = _scan_order(n_sub, rev)

    log_i = pltpu.roll(gc, 2 * N_HEADS, axis=1)
    g = _chunk_rows(a_cols, last) - a_cols + log_i
    m_loc = [jnp.max(g[j * CHUNK:(j + 1) * CHUNK], axis=0, keepdims=True) for j in range(n_sub)]
    w_in = jnp.exp(g - jnp.concatenate([jnp.broadcast_to(r, (CHUNK, LANES)) for r in m_loc], axis=0))
    m = m_ref[d]
    m_before, s_old, s_new = [None] * n_sub, [None] * n_sub, [None] * n_sub
    for j in order:
        a_end = a_cols[j * CHUNK + last:j * CHUNK + last + 1]
        m_before[j] = m
        m_new = jnp.maximum(a_end + m, m_loc[j])
        s_old[j] = jnp.exp(a_end + m - m_new)
        s_new[j] = jnp.exp(m_loc[j] - m_new)
        m = m_new
    m_ref[d] = m
    m_prev_all = jnp.concatenate([jnp.broadcast_to(r, (CHUNK, LANES)) for r in m_before], axis=0)

    for h in range(N_HEADS):
        ic = d * N_HEADS + h
        fc = 2 * N_HEADS + d * N_HEADS + h
        q_h = qk[:, h * D_HEAD:(h + 1) * D_HEAD]
        k_h = qk[:, MIX_W + h * D_HEAD:MIX_W + (h + 1) * D_HEAD]
        v_h = vb[:, h * D_HEAD:(h + 1) * D_HEAD]
        qb = q_h.astype(BF16)
        wk = w_in[:, fc:fc + 1] * k_h
        wkb = wk.astype(BF16)
        c_st = c_ref[d, h]
        n_st = n_ref[d, h]
        inter, qn = [None] * n_sub, [None] * n_sub
        for j in order:
            rows = slice(j * CHUNK, (j + 1) * CHUNK)
            if want_out:
                inter[j] = _mm(qb[rows], c_st.astype(BF16))
                qn[j] = jnp.sum(q_h[rows] * n_st, axis=1, keepdims=True)
            so = s_old[j][:, fc:fc + 1]
            sn = s_new[j][:, fc:fc + 1]
            c_st = so * c_st + sn * _mm(wkb[rows], v_h[rows], TN_DIMS)
            n_st = so * n_st + sn * jnp.sum(wk[rows], axis=0, keepdims=True)
        c_ref[d, h] = c_st
        n_ref[d, h] = n_st
        if not want_out:
            continue

        r_row = gr[ic:ic + 1, :] - a_rows[fc:fc + 1, :]
        m_prev = m_prev_all[:, fc:fc + 1]
        r_max = jnp.max(jnp.where(mask, r_row, -jnp.inf), axis=1, keepdims=True)
        mm = jnp.maximum(r_max, m_prev)
        w_intra = _mm(qb, k_h.astype(BF16), NT_DIMS) * jnp.exp(jnp.where(mask, r_row - mm, -jnp.inf))
        w_inter = jnp.exp(m_prev - mm)
        num = _mm(w_intra.astype(BF16), v_h) + w_inter * jnp.concatenate(inter, axis=0)
        den = jnp.sum(w_intra, axis=1, keepdims=True) + w_inter * jnp.concatenate(qn, axis=0)
        floor = jnp.exp(-(a_cols[:, fc:fc + 1] + mm))
        o_ref[0, :, MIX_W + h * D_HEAD:MIX_W + (h + 1) * D_HEAD] = num / jnp.maximum(jnp.abs(den), floor)


def _mix_kernel(n_cb, lbl_ref,
                hq_f, hi_f, hf_f, mv_f, qk_f, gc_f, gr_f,
                hq_b, hi_b, hf_b, mv_b, qk_b, gc_b, gr_b,
                of_ref, ob_ref, st_ref, c_ref, n_ref, m_ref):
    s = pl.program_id(1)

    @pl.when(s == 0)
    def _():
        st_ref[...] = jnp.zeros_like(st_ref)
        c_ref[...] = jnp.zeros_like(c_ref)
        n_ref[...] = jnp.zeros_like(n_ref)
        m_ref[...] = jnp.zeros_like(m_ref)

    def step(want_out):
        for d, rev, (hq, hi, hf, mv, qk, gc, gr, o_ref) in (
                (0, False, (hq_f, hi_f, hf_f, mv_f, qk_f, gc_f, gr_f, of_ref)),
                (1, True, (hq_b, hi_b, hf_b, mv_b, qk_b, gc_b, gr_b, ob_ref))):
            logits = lbl_ref[d]
            mx = jnp.max(logits, axis=0, keepdims=True)
            ex = jnp.exp(logits - mx)
            lb = ex[0:1, :] / jnp.sum(ex, axis=0, keepdims=True)
            _hgrn_dir(d, rev, want_out, hq, hi, hf, lb, st_ref, o_ref)
            _mlstm_dir(d, rev, want_out, qk, mv, gc, gr, c_ref, n_ref, m_ref, o_ref)

    @pl.when(s < n_cb)
    def _():
        step(False)

    @pl.when(s >= n_cb)
    def _():
        step(True)


def _token_mixers(p, qk, gates, gates_t, lb_logits, n_ctx):
    bsz, t_all, _ = p.shape
    n_lat = t_all - n_ctx
    n_cb = n_ctx // MIX_BLOCK
    n_blk = t_all // MIX_BLOCK

    def f_idx(s):
        return s

    def b_idx(s):
        return jnp.where(s < n_cb, n_cb - 1 - s, n_blk - 1 + n_cb - s)

    def pspec(col, idx):
        return pl.BlockSpec((1, MIX_BLOCK, MIX_W), lambda b, s: (b, idx(s), col))

    def dir_specs(idx, col_f):
        return [pspec(COL_HQ, idx), pspec(COL_HI, idx), pspec(col_f, idx), pspec(COL_MV, idx),
                pl.BlockSpec((1, MIX_BLOCK, 2 * MIX_W), lambda b, s: (b, idx(s), 0)),
                pl.BlockSpec((1, MIX_BLOCK, LANES), lambda b, s: (b, idx(s), 0)),
                pl.BlockSpec((1, N_GATES, MIX_BLOCK), lambda b, s: (b, 0, idx(s)))]

    out_f = pl.BlockSpec((1, MIX_BLOCK, 2 * MIX_W), lambda b, s: (b, jnp.maximum(s - n_cb, 0), 0))
    out_b = pl.BlockSpec((1, MIX_BLOCK, 2 * MIX_W),
                         lambda b, s: (b, jnp.minimum(n_blk - 1 - s, n_blk - 1 - n_cb), 0))
    args_dir = [p, p, p, p, qk, gates, gates_t]
    return pl.pallas_call(
        functools.partial(_mix_kernel, n_cb),
        grid=(bsz, n_blk),
        in_specs=[pl.BlockSpec((2, 2, MIX_W), lambda b, s: (0, 0, 0))]
        + dir_specs(f_idx, COL_HF0) + dir_specs(b_idx, COL_HF1),
        out_specs=[out_f, out_b],
        out_shape=[jax.ShapeDtypeStruct((bsz, n_lat, 2 * MIX_W), F32)] * 2,
        scratch_shapes=[pltpu.VMEM((2, N_HEADS, D_HEAD, D_HEAD), F32),
                        pltpu.VMEM((2, N_HEADS, D_HEAD, D_HEAD), F32),
                        pltpu.VMEM((2, N_HEADS, 1, D_HEAD), F32),
                        pltpu.VMEM((2, 1, LANES), F32)],
        compiler_params=_cparams(("arbitrary", "arbitrary")),
        name="mix",
    )(lb_logits, *args_dir, *args_dir)


def _out_kernel(of_ref, ob_ref, hg_ref, mo_ref, x_ref, mod_ref, w_ref, hgn_ref, mln_ref, lng_ref, lnb_ref,
                x1_ref, h2t_ref, h2t_bf_ref):
    o = of_ref[0] + ob_ref[0]
    hg = hg_ref[0]
    mo = mo_ref[0]
    parts = []
    for h in range(N_HEADS):
        sl = slice(h * D_HEAD, (h + 1) * D_HEAD)
        oh = o[:, sl]
        y = oh * lax.rsqrt(jnp.mean(oh * oh, axis=-1, keepdims=True) + LN_EPS) * hgn_ref[:, sl]
        parts.append(y * _silu(hg[:, sl]))
    for h in range(N_HEADS):
        sl = slice(h * D_HEAD, (h + 1) * D_HEAD)
        oh = o[:, MIX_W + h * D_HEAD:MIX_W + (h + 1) * D_HEAD]
        parts.append(_ln(oh) * mln_ref[:, sl] * _sigmoid(mo[:, sl]))
    mix = jnp.concatenate(parts, axis=-1).astype(BF16)
    y = _mm(mix, w_ref[...])
    g1 = mod_ref[0, :, 0:D_MODEL]
    sh2 = mod_ref[0, :, D_MODEL:2 * D_MODEL]
    sc2 = mod_ref[0, :, 2 * D_MODEL:3 * D_MODEL]
    x1 = _ln(DEEPNORM_ALPHA * x_ref[0] + g1 * y) * lng_ref[...] + lnb_ref[...]
    x1_ref[0] = x1
    h2t = (_ln(x1) * (1.0 + sc2) + sh2).T
    h2t_ref[...] = h2t
    h2t_bf_ref[...] = h2t.astype(BF16)


def _readout_project(o_f, o_b, p, x, mod2, w_out, hg_norm_g, ml_norm_g, ln1_g, ln1_b, n_ctx):
    bsz, n_lat, _ = x.shape
    n_tiles = n_lat // ROW_TILE
    off = n_ctx // ROW_TILE
    row = lambda a: a.reshape(1, -1)
    const2 = lambda b, t: (0, 0)
    return pl.pallas_call(
        _out_kernel,
        grid=(bsz, n_tiles),
        in_specs=[pl.BlockSpec((1, ROW_TILE, 2 * MIX_W), lambda b, t: (b, t, 0)),
                  pl.BlockSpec((1, ROW_TILE, 2 * MIX_W), lambda b, t: (b, t, 0)),
                  pl.BlockSpec((1, ROW_TILE, MIX_W), lambda b, t: (b, t + off, COL_HG)),
                  pl.BlockSpec((1, ROW_TILE, MIX_W), lambda b, t: (b, t + off, COL_MO)),
                  pl.BlockSpec((1, ROW_TILE, D_MODEL), lambda b, t: (b, t, 0)),
                  pl.BlockSpec((1, 1, 3 * D_MODEL), lambda b, t: (b, 0, 0)),
                  pl.BlockSpec((2 * MIX_W, D_MODEL), const2),
                  pl.BlockSpec((1, MIX_W), const2),
                  pl.BlockSpec((1, MIX_W), const2),
                  pl.BlockSpec((1, D_MODEL), const2),
                  pl.BlockSpec((1, D_MODEL), const2)],
        out_specs=[pl.BlockSpec((1, ROW_TILE, D_MODEL), lambda b, t: (b, t, 0)),
                   pl.BlockSpec((D_MODEL, ROW_TILE), lambda b, t: (0, b * n_tiles + t)),
                   pl.BlockSpec((D_MODEL, ROW_TILE), lambda b, t: (0, b * n_tiles + t))],
        out_shape=[jax.ShapeDtypeStruct((bsz, n_lat, D_MODEL), F32),
                   jax.ShapeDtypeStruct((D_MODEL, bsz * n_lat), F32),
                   jax.ShapeDtypeStruct((D_MODEL, bsz * n_lat), BF16)],
        compiler_params=_cparams(("arbitrary", "arbitrary")),
        name="readout",
    )(o_f, o_b, p, p, x, mod2, w_out.astype(BF16), row(hg_norm_g), row(ml_norm_g), row(ln1_g), row(ln1_b))


def _extract_exact(s):
    ridx = lax.broadcasted_iota(jnp.int32, s.shape, 0).astype(F32)
    rank = jnp.full(s.shape, float(PEER_TOPK), F32)
    vals = []
    for it in range(PEER_TOPK):
        m = jnp.max(s, axis=0, keepdims=True)
        first = jnp.min(jnp.where(s == m, ridx, float(s.shape[0])), axis=0, keepdims=True)
        hit = ridx == first
        rank = jnp.where(hit, float(it), rank)
        s = jnp.where(hit, -jnp.inf, s)
        vals.append(m)
    return rank, vals


def _extract_fast(s):
    rank = jnp.full(s.shape, float(PEER_TOPK), F32)
    vals = []
    for it in range(PEER_TOPK):
        m = jnp.max(s, axis=0, keepdims=True)
        hit = s == m
        rank = jnp.where(hit, float(it), rank)
        s = jnp.where(hit, -jnp.inf, s)
        vals.append(m)
    return rank, vals


def _count_true(mask):
    return jnp.sum(jnp.where(mask, 1.0, 0.0), axis=0, keepdims=True)


def _cand_grid(v1, v2):
    v1a = jnp.concatenate(v1, axis=0)
    v2a = jnp.concatenate(v2, axis=0)
    row = lax.broadcasted_iota(jnp.int32, (8, v1a.shape[1]), 0)
    blocks = [v1a[0:1] + v2a]
    for a in range(1, 8):
        nb = PEER_TOPK // (a + 1)
        blk = v1a[a:a + 1] + v2a[0:8]
        blocks.append(blk if nb >= 8 else jnp.where(row < nb, blk, -jnp.inf))
    blocks.append(v1a[8:16] + v2a[0:1])
    return jnp.concatenate(blocks, axis=0)


def _cand_counts(chosen):
    cnt = [_count_true(chosen[0:PEER_TOPK])]
    for a in range(1, 8):
        cnt.append(_count_true(chosen[PEER_TOPK + 8 * (a - 1):PEER_TOPK + 8 * a]))
    base = PEER_TOPK + 8 * 7
    for r in range(8):
        cnt.append(jnp.where(chosen[base + r:base + r + 1], 1.0, 0.0))
    return cnt


def _sel_emit(hd, s1, s2, is_rank1, rank2, v1, v2, cand, chosen, r2_ref, ci_ref, a_ref, b_ref):
    top = v1[0] + v2[0]
    z = jnp.sum(jnp.where(chosen, jnp.exp(cand - top), 0.0), axis=0, keepdims=True)
    ci = jnp.zeros_like(s1)
    for a, cnt_a in enumerate(_cand_counts(chosen)):
        ci = jnp.where(is_rank1(a), cnt_a, ci)
    r2_ref[hd] = rank2.astype(BF16)
    ci_ref[hd] = ci
    a_ref[hd] = jnp.exp(s1 - v1[0])
    b_ref[hd] = (jnp.exp(s2 - v2[0]) / z).astype(BF16)


def _sel_kernel(h_ref, wq_hi_ref, wq_lo_ref, k_hi_ref, k_lo_ref, r2_ref, ci_ref, a_ref, b_ref):
    h2t = h_ref[...]
    h_hi, h_lo = _split(h2t, 2)
    qt = (_mm(wq_hi_ref[...], h_hi) + (_mm(wq_lo_ref[...], h_hi) + _mm(wq_hi_ref[...], h_lo)))
    half = PEER_DQ // 2
    for hd in range(PEER_HEADS):
        s = []
        for p_ in range(2):
            qh = qt[hd * PEER_DQ + p_ * half: hd * PEER_DQ + (p_ + 1) * half, :]
            q_hi, q_lo = _split(qh, 2)
            k_hi = k_hi_ref[hd, p_]
            k_lo = k_lo_ref[hd, p_]
            s.append(_mm(k_hi, q_hi) + (_mm(k_lo, q_hi) + _mm(k_hi, q_lo)))
        s1, s2 = s

        _, v1 = _extract_fast(s1)
        rank2, v2 = _extract_fast(s2)
        cand = _cand_grid(v1, v2)
        _, cv = _extract_fast(cand)
        chosen = cand >= cv[PEER_TOPK - 1]
        k = float(PEER_TOPK)
        tied = ((_count_true(s1 >= v1[PEER_TOPK - 1]) != k) | (_count_true(rank2 < k) != k)
                | (_count_true(chosen) != k))
        any_tied = jnp.max(jnp.where(tied, 1.0, 0.0)) > 0.0
        _sel_emit(hd, s1, s2, lambda a: s1 == v1[a], rank2, v1, v2, cand, chosen, r2_ref, ci_ref, a_ref, b_ref)

        @pl.when(any_tied)
        def _():
            rank1, v1 = _extract_exact(s1)
            rank2, v2 = _extract_exact(s2)
            cand = _cand_grid(v1, v2)
            crank, _ = _extract_exact(cand)
            _sel_emit(hd, s1, s2, lambda a: rank1 == float(a), rank2, v1, v2, cand, crank < k,
                      r2_ref, ci_ref, a_ref, b_ref)


def _peer_select(h2t, peer_wq, peer_keys):
    n_tok = h2t.shape[1]
    wqt = peer_wq.T
    wq_hi = wqt.astype(BF16)
    wq_lo = (wqt - wq_hi.astype(F32)).astype(BF16)
    k_hi = peer_keys.astype(BF16)
    k_lo = (peer_keys - k_hi.astype(F32)).astype(BF16)
    dq_all = PEER_HEADS * PEER_DQ
    out_spec = pl.BlockSpec((PEER_HEADS, PEER_NKEYS, SEL_TILE), lambda t: (0, 0, t))
    shape = (PEER_HEADS, PEER_NKEYS, n_tok)
    return pl.pallas_call(
        _sel_kernel,
        grid=(n_tok // SEL_TILE,),
        in_specs=[pl.BlockSpec((D_MODEL, SEL_TILE), lambda t: (0, t)),
                  pl.BlockSpec((dq_all, D_MODEL), lambda t: (0, 0)),
                  pl.BlockSpec((dq_all, D_MODEL), lambda t: (0, 0)),
                  pl.BlockSpec((PEER_HEADS, 2, PEER_NKEYS, PEER_DQ // 2), lambda t: (0, 0, 0, 0)),
                  pl.BlockSpec((PEER_HEADS, 2, PEER_NKEYS, PEER_DQ // 2), lambda t: (0, 0, 0, 0))],
        out_specs=[out_spec] * 4,
        out_shape=[jax.ShapeDtypeStruct(shape, BF16), jax.ShapeDtypeStruct(shape, F32),
                   jax.ShapeDtypeStruct(shape, F32), jax.ShapeDtypeStruct(shape, BF16)],
        compiler_params=_cparams(("arbitrary",)),
        name="peersel",
    )(h2t, wq_hi, wq_lo, k_hi, k_lo)


def _rows_bf16(row):
    tile = jnp.broadcast_to(row, (16, row.shape[1])).astype(BF16)
    return jnp.concatenate([tile] * (PEER_NKEYS // 16), axis=0)


def _peer_kernel(n_e, h_ref, r2_ref, b_ref, ci_ref, a_ref, u_ref, vt_ref, x1_ref, g2_ref, lng_ref, lnb_ref,
                 o_ref, acc_ref, pre_ref, wa_ref):
    e = pl.program_id(1)

    @pl.when(e == 0)
    def _():
        acc_ref[...] = jnp.zeros_like(acc_ref)

    pre_ref[...] = _mm(u_ref[...], h_ref[...])
    zero = jnp.zeros((), BF16)
    for ii in range(PEER_ET // PEER_NKEYS):
        w = None
        for hd in range(PEER_HEADS):
            cnt = _rows_bf16(ci_ref[hd, ii:ii + 1, :])
            fac = _rows_bf16(a_ref[hd, ii:ii + 1, :])
            term = jnp.where(r2_ref[hd] < cnt, b_ref[hd] * fac, zero)
            w = term if w is None else w + term
        rows = pl.ds(ii * PEER_NKEYS, PEER_NKEYS)
        wa_ref[rows, :] = w * _gelu(pre_ref[rows, :].astype(BF16))
    acc_ref[...] += _mm(vt_ref[...], wa_ref[...])

    @pl.when(e == n_e - 1)
    def _():
        y = acc_ref[...].T
        o_ref[0] = _ln(DEEPNORM_ALPHA * x1_ref[0] + g2_ref[0] * y) * lng_ref[...] + lnb_ref[...]


def _peer_dense(h2t_bf, r2, ci, a_fac, b_fac, u_bf, vt_bf, x1, g2, ln2_g, ln2_b):
    bsz, n_lat, _ = x1.shape
    n_tok = bsz * n_lat
    n_t = n_tok // PEER_TT
    per_b = n_lat // PEER_TT
    n_e = u_bf.shape[0] // PEER_ET
    per = PEER_ET // PEER_NKEYS
    assert per % 8 == 0, "whole sublane tiles of half-1 keys per expert step"
    tok_spec = pl.BlockSpec((PEER_HEADS, PEER_NKEYS, PEER_TT), lambda t, e: (0, 0, t))
    key_spec = pl.BlockSpec((PEER_HEADS, per, PEER_TT), lambda t, e: (0, e, t))
    const2 = lambda t, e: (0, 0)
    return pl.pallas_call(
        functools.partial(_peer_kernel, n_e),
        grid=(n_t, n_e),
        in_specs=[pl.BlockSpec((D_MODEL, PEER_TT), lambda t, e: (0, t)),
                  tok_spec, tok_spec, key_spec, key_spec,
                  pl.BlockSpec((PEER_ET, D_MODEL), lambda t, e: (e, 0)),
                  pl.BlockSpec((D_MODEL, PEER_ET), lambda t, e: (0, e)),
                  pl.BlockSpec((1, PEER_TT, D_MODEL), lambda t, e: (t // per_b, t % per_b, 0)),
                  pl.BlockSpec((1, 1, D_MODEL), lambda t, e: (t // per_b, 0, 0)),
                  pl.BlockSpec((1, D_MODEL), const2),
                  pl.BlockSpec((1, D_MODEL), const2)],
        out_specs=pl.BlockSpec((1, PEER_TT, D_MODEL), lambda t, e: (t // per_b, t % per_b, 0)),
        out_shape=jax.ShapeDtypeStruct((bsz, n_lat, D_MODEL), F32),
        scratch_shapes=[pltpu.VMEM((D_MODEL, PEER_TT), F32),
                        pltpu.VMEM((PEER_ET, PEER_TT), F32),
                        pltpu.VMEM((PEER_ET, PEER_TT), BF16)],
        compiler_params=_cparams(("arbitrary", "arbitrary")),
        name="peer",
    )(h2t_bf, r2, b_fac, ci, a_fac, u_bf, vt_bf, x1, g2, ln2_g.reshape(1, -1), ln2_b.reshape(1, -1))


def kernel(x, c, ctx, c_ctx, w_mod, b_mod, w_in, hg_lb_logits, hg_norm_g, ml_conv_w, ml_conv_b, ml_gate_b,
           ml_norm_g, w_out, ln1_g, ln1_b, peer_wq, peer_keys, peer_u, peer_v, ln2_g, ln2_b):
    bsz, n_lat, _ = x.shape
    n_ctx = ctx.shape[1]
    assert n_ctx % ROW_TILE == 0 and n_lat % ROW_TILE == 0 and ROW_TILE % CHUNK == 0
    assert n_ctx % MIX_BLOCK == 0 and n_lat % MIX_BLOCK == 0 and MIX_BLOCK % CHUNK == 0
    assert w_mod.shape[0] == 1, "single-layer kernel"
    lyr = 0

    n_rows = -(-(bsz + 1) // 8) * 8
    c_all = jnp.zeros((n_rows, D_MODEL), F32).at[:bsz].set(c).at[bsz].set(c_ctx)
    mod = _modulation(c_all, w_mod[lyr], b_mod[lyr])
    mod_l, mod_c = mod[:bsz], mod[bsz]
    n_tiles_ctx = n_ctx // ROW_TILE
    n_tiles = (n_ctx + n_lat) // ROW_TILE
    mod1_l = mod_l[:, None, :2 * D_MODEL]
    mod1_c = jnp.broadcast_to(mod_c[None, None, :2 * D_MODEL], (bsz, 1, 2 * D_MODEL))
    mod1 = jnp.concatenate([jnp.repeat(mod1_c, n_tiles_ctx, axis=1),
                            jnp.repeat(mod1_l, n_tiles - n_tiles_ctx, axis=1)], axis=1)[:, :, None, :]
    mod2 = mod_l[:, None, 2 * D_MODEL:5 * D_MODEL]
    g2 = mod_l[:, None, 5 * D_MODEL:6 * D_MODEL]

    w_main = w_in[lyr][:, :D_MAIN].astype(BF16)
    w_gates = w_in[lyr][:, D_MAIN:]
    p, gates, gates_t = _input_projection(ctx, x, mod1, w_main, w_gates, ml_gate_b[lyr])

    qk = _qk_conv(p, ml_conv_w[lyr], ml_conv_b[lyr], n_ctx)
    o_f, o_b = _token_mixers(p, qk, gates, gates_t, hg_lb_logits[:, lyr:lyr + 2], n_ctx)
    x1, h2t, h2t_bf = _readout_project(o_f, o_b, p, x, mod2, w_out[lyr], hg_norm_g[lyr], ml_norm_g[lyr],
                                       ln1_g[lyr], ln1_b[lyr], n_ctx)
    r2, ci, a_fac, b_fac = _peer_select(h2t, peer_wq[lyr], peer_keys[lyr])
    u_bf = peer_u[lyr].astype(BF16)
    vt_bf = peer_v[lyr].T.astype(BF16)
    return _peer_dense(h2t_bf, r2, ci, a_fac, b_fac, u_bf, vt_bf, x1, g2, ln2_g[lyr], ln2_b[lyr])
```

```python
import functools
import math

import jax
import jax.numpy as jnp
from jax import lax
from jax.experimental import pallas as pl
from jax.experimental.pallas import tpu as pltpu

F32 = jnp.float32
BF16 = jnp.bfloat16

D_MODEL = 1024
CHUNK = 64
GRID_W = 64
N_HEADS = 4
D_HEAD = 128
MIX_W = N_HEADS * D_HEAD
N_GATES = 4 * N_HEADS
D_MAIN = 9 * MIX_W
PEER_HEADS = 8
PEER_NKEYS = 128
PEER_TOPK = 16
PEER_DQ = 256
LN_EPS = 1e-6
DEEPNORM_ALPHA = 2.0 ** 0.25
LANES = 128
VMEM_LIMIT = 56 * 1024 * 1024

COL_HQ, COL_HI, COL_HG, COL_HF0, COL_HF1, COL_MQ, COL_MK, COL_MV, COL_MO = range(9)

ROW_TILE = 256
SEL_TILE = 512
PEER_TT = 512
PEER_ET = 2048

NT_DIMS = (((1,), (1,)), ((), ()))
TN_DIMS = (((0,), (0,)), ((), ()))


def _cparams(sem):
    return pltpu.CompilerParams(dimension_semantics=sem, vmem_limit_bytes=VMEM_LIMIT)


def _split(a, n):
    parts = []
    r = a
    for k in range(n):
        p = r.astype(BF16)
        parts.append(p)
        if k + 1 < n:
            r = r - p.astype(F32)
    return parts


def _mm(a, b, dims=None):
    if dims is None:
        return jnp.dot(a, b, preferred_element_type=F32)
    return lax.dot_general(a, b, dims, preferred_element_type=F32)


def _mm_bf16(a, b, dims=None):
    return _mm(a.astype(BF16), b.astype(BF16), dims)


def _mm_x3(a, b, dims=None):
    a_hi, a_lo = _split(a, 2)
    b_hi, b_lo = _split(b, 2)
    return _mm(a_hi, b_hi, dims) + (_mm(a_hi, b_lo, dims) + _mm(a_lo, b_hi, dims))


def _mm_x3_pre(a, b_hi, b_lo, dims=None):
    a_hi, a_lo = _split(a, 2)
    return _mm(a_hi, b_hi, dims) + (_mm(a_hi, b_lo, dims) + _mm(a_lo, b_hi, dims))


def _mm_exact_left(m_bf16, x):
    x0, x1, x2 = _split(x, 3)
    return _mm(m_bf16, x0) + (_mm(m_bf16, x1) + _mm(m_bf16, x2))


def _mm_exact_right(x, m_bf16):
    x0, x1, x2 = _split(x, 3)
    return _mm(x0, m_bf16) + (_mm(x1, m_bf16) + _mm(x2, m_bf16))


def _sigmoid(x):
    return 1.0 / (1.0 + jnp.exp(-x))


def _silu(x):
    return x * _sigmoid(x)


def _log_sigmoid(x):
    return jnp.minimum(x, 0.0) - jnp.log(1.0 + jnp.exp(-jnp.abs(x)))


def _ln(x):
    mu = jnp.mean(x, axis=-1, keepdims=True)
    xc = x - mu
    var = jnp.mean(xc * xc, axis=-1, keepdims=True)
    return xc * lax.rsqrt(var + LN_EPS)


def _gelu(x):
    return 0.5 * x * (1.0 + lax.erf(x * (2.0 ** -0.5)))


def _mod_kernel(c_ref, w_ref, b_ref, o_ref):
    s = _silu(c_ref[...])
    o_ref[...] = _mm_x3(s, w_ref[...]) + b_ref[...]


def _modulation(c_all, w_mod, b_mod):
    n = c_all.shape[0]
    d_out = w_mod.shape[1]
    blk = 1024
    return pl.pallas_call(
        _mod_kernel,
        grid=(d_out // blk,),
        in_specs=[pl.BlockSpec((n, D_MODEL), lambda j: (0, 0)),
                  pl.BlockSpec((D_MODEL, blk), lambda j: (0, j)),
                  pl.BlockSpec((1, blk), lambda j: (0, j))],
        out_specs=pl.BlockSpec((n, blk), lambda j: (0, j)),
        out_shape=jax.ShapeDtypeStruct((n, d_out), F32),
        compiler_params=_cparams(("arbitrary",)),
        name="mod",
    )(c_all, w_mod, b_mod.reshape(1, d_out))


def _inproj_kernel(n_ctx_tiles, ctx_ref, x_ref, mod_ref, w_ref, wg_hi_ref, wg_lo_ref, gb_ref,
                   p_ref, g_ref, gt_ref):
    x = jnp.where(pl.program_id(1) < n_ctx_tiles, ctx_ref[0], x_ref[0])
    shift = mod_ref[0, 0, :, :D_MODEL]
    scale = mod_ref[0, 0, :, D_MODEL:]
    h = _ln(x) * (1.0 + scale) + shift
    p_ref[0] = _mm(h.astype(BF16), w_ref[...])
    h_hi, h_lo = _split(h, 2)
    g = _mm(h_hi, wg_hi_ref[...]) + (_mm(h_hi, wg_lo_ref[...]) + _mm(h_lo, wg_hi_ref[...])) + gb_ref[...]
    g_ref[0] = g
    gt_ref[0] = g.T[:N_GATES, :]


def _input_projection(ctx, x, mod1, w_main, wg, gate_b):
    bsz, n_ctx, _ = ctx.shape
    n_ctx_tiles = n_ctx // ROW_TILE
    t_all = n_ctx + x.shape[1]
    n_tiles = t_all // ROW_TILE
    wg_pad = jnp.zeros((D_MODEL, LANES), F32).at[:, :N_GATES].set(wg)
    wg_hi = wg_pad.astype(BF16)
    wg_lo = (wg_pad - wg_hi.astype(F32)).astype(BF16)
    gb = jnp.zeros((1, LANES), F32).at[0, :N_GATES].set(gate_b)
    const2 = lambda b, t: (0, 0)
    return pl.pallas_call(
        functools.partial(_inproj_kernel, n_ctx_tiles),
        grid=(bsz, n_tiles),
        in_specs=[pl.BlockSpec((1, ROW_TILE, D_MODEL), lambda b, t: (b, jnp.minimum(t, n_ctx_tiles - 1), 0)),
                  pl.BlockSpec((1, ROW_TILE, D_MODEL), lambda b, t: (b, jnp.maximum(t - n_ctx_tiles, 0), 0)),
                  pl.BlockSpec((1, 1, 1, 2 * D_MODEL), lambda b, t: (b, t, 0, 0)),
                  pl.BlockSpec((D_MODEL, D_MAIN), const2),
                  pl.BlockSpec((D_MODEL, LANES), const2),
                  pl.BlockSpec((D_MODEL, LANES), const2),
                  pl.BlockSpec((1, LANES), const2)],
        out_specs=[pl.BlockSpec((1, ROW_TILE, D_MAIN), lambda b, t: (b, t, 0)),
                   pl.BlockSpec((1, ROW_TILE, LANES), lambda b, t: (b, t, 0)),
                   pl.BlockSpec((1, N_GATES, ROW_TILE), lambda b, t: (b, 0, t))],
        out_shape=[jax.ShapeDtypeStruct((bsz, t_all, D_MAIN), F32),
                   jax.ShapeDtypeStruct((bsz, t_all, LANES), F32),
                   jax.ShapeDtypeStruct((bsz, N_GATES, t_all), F32)],
        compiler_params=_cparams(("arbitrary", "arbitrary")),
        name="inproj",
    )(ctx, x, mod1, w_main, wg_hi, wg_lo, gb)


CONV_PAD = 72
CONV_CB = 256


def _conv_kernel(n_ctx, n_lat, p_ref, w_ref, b_ref, o_ref, pad_c, pad_l):
    cb = pl.program_id(1)
    w = w_ref[...]
    bias = b_ref[...]
    scale = jnp.where(cb >= MIX_W // CONV_CB, D_HEAD ** -0.5, 1.0).astype(F32)

    def finish(y):
        return _silu(y + bias) * scale

    pad_c[...] = jnp.zeros_like(pad_c)
    pad_c[pl.ds(CONV_PAD, n_ctx), :] = p_ref[0, pl.ds(0, n_ctx), :]
    y = jnp.zeros((n_ctx, CONV_CB), F32)
    for dc in range(3):
        y = y + pad_c[pl.ds(CONV_PAD + dc - 1, n_ctx), :] * w[3 + dc:4 + dc, :]
    o_ref[0, pl.ds(0, n_ctx), :] = finish(y)

    x = p_ref[0, pl.ds(n_ctx, n_lat), :]
    col = lax.broadcasted_iota(jnp.int32, (n_lat, CONV_CB), 0) & (GRID_W - 1)
    pad_l[...] = jnp.zeros_like(pad_l)
    pad_l[0, pl.ds(CONV_PAD + 1, n_lat), :] = jnp.where(col == GRID_W - 1, 0.0, x)
    pad_l[1, pl.ds(CONV_PAD, n_lat), :] = x
    pad_l[2, pl.ds(CONV_PAD - 1, n_lat), :] = jnp.where(col == 0, 0.0, x)
    y = jnp.zeros((n_lat, CONV_CB), F32)
    for dr in range(3):
        for dc in range(3):
            rows = pl.ds(CONV_PAD + (dr - 1) * GRID_W, n_lat)
            y = y + pad_l[dc, rows, :] * w[3 * dr + dc:3 * dr + dc + 1, :]
    o_ref[0, pl.ds(n_ctx, n_lat), :] = finish(y)


def _qk_conv(p, conv_w, conv_b, n_ctx):
    bsz, t_all, _ = p.shape
    n_lat = t_all - n_ctx
    n_cb = 2 * MIX_W // CONV_CB
    first = COL_MQ * MIX_W // CONV_CB
    w9 = conv_w.reshape(9, 2 * MIX_W)
    return pl.pallas_call(
        functools.partial(_conv_kernel, n_ctx, n_lat),
        grid=(bsz, n_cb),
        in_specs=[pl.BlockSpec((1, t_all, CONV_CB), lambda b, c: (b, 0, first + c)),
                  pl.BlockSpec((9, CONV_CB), lambda b, c: (0, c)),
                  pl.BlockSpec((1, CONV_CB), lambda b, c: (0, c))],
        out_specs=pl.BlockSpec((1, t_all, CONV_CB), lambda b, c: (b, 0, c)),
        out_shape=jax.ShapeDtypeStruct((bsz, t_all, 2 * MIX_W), F32),
        scratch_shapes=[pltpu.VMEM((n_ctx + 2 * CONV_PAD, CONV_CB), F32),
                        pltpu.VMEM((3, n_lat + 2 * CONV_PAD, CONV_CB), F32)],
        compiler_params=_cparams(("arbitrary", "arbitrary")),
        name="qkconv",
    )(p, w9, conv_b.reshape(1, 2 * MIX_W))


MIX_BLOCK = 256


def _block_mask(t, rev):
    r = lax.broadcasted_iota(jnp.int32, (t, t), 0)
    c = lax.broadcasted_iota(jnp.int32, (t, t), 1)
    shift = CHUNK.bit_length() - 1
    same = (r >> shift) == (c >> shift)
    return jnp.logical_and(same, (c >= r) if rev else (c <= r))


def _chunk_rows(x, idx):
    return jnp.concatenate(
        [jnp.broadcast_to(x[j * CHUNK + idx:j * CHUNK + idx + 1], (CHUNK, x.shape[1]))
         for j in range(x.shape[0] // CHUNK)], axis=0)


def _scan_order(n, rev):
    return range(n - 1, -1, -1) if rev else range(n)


def _hgrn_dir(d, rev, want_out, q_ref, v_ref, z_ref, lb, st_ref, o_ref):
    vb = v_ref[0].astype(BF16)
    sig = _sigmoid(z_ref[0])
    log_f = jnp.log(lb + (1.0 - lb) * sig)
    kk = (1.0 - lb) * (1.0 - sig)
    t = vb.shape[0]
    n_sub = t // CHUNK
    mask = _block_mask(t, rev)
    tri = jnp.where(mask, 1.0, 0.0).astype(BF16)
    b = _mm_exact_left(tri, log_f)
    last = 0 if rev else CHUNK - 1
    mid = CHUNK // 2 - 1 if rev else CHUNK // 2
    kdec = (kk * jnp.exp(_chunk_rows(b, last) - b)).astype(BF16)
    if want_out:
        q = q_ref[0]
        b_mid = _chunk_rows(b, mid)
        qd = (q * jnp.exp(b - b_mid)).astype(BF16)
        kd = (kk * jnp.exp(b_mid - b)).astype(BF16)
        qe = (q * jnp.exp(b)).astype(BF16)
    for h in range(N_HEADS):
        sl = slice(h * D_HEAD, (h + 1) * D_HEAD)
        st = st_ref[d, h]
        inter = [None] * n_sub
        for j in _scan_order(n_sub, rev):
            rows = slice(j * CHUNK, (j + 1) * CHUNK)
            if want_out:
                inter[j] = _mm(qe[rows, sl], st.astype(BF16), NT_DIMS)
            e_last = jnp.exp(b[j * CHUNK + last:j * CHUNK + last + 1, sl])
            st = st * e_last + _mm(vb[rows, sl], kdec[rows, sl], TN_DIMS)
        st_ref[d, h] = st
        if want_out:
            sc = _mm(qd[:, sl], kd[:, sl], NT_DIMS)
            o = _mm(jnp.where(mask, sc, 0.0).astype(BF16), vb[:, sl])
            o_ref[0, :, sl] = o + jnp.concatenate(inter, axis=0)


def _mlstm_dir(d, rev, want_out, qk_ref, v_ref, gc_ref, gr_ref, c_ref, n_ref, m_ref, o_ref):
    qk = qk_ref[0]
    vb = v_ref[0].astype(BF16)
    gc = gc_ref[0]
    gr = gr_ref[0]
    t = gc.shape[0]
    n_sub = t // CHUNK
    mask = _block_mask(t, rev)
    tri = jnp.where(mask, 1.0, 0.0).astype(BF16)
    a_cols = _mm_exact_left(tri, _log_sigmoid(gc))
    if want_out:
        tri_t = jnp.where(_block_mask(t, not rev), 1.0, 0.0).astype(BF16)
        a_rows = _mm_exact_right(_log_sigmoid(gr), tri_t)
    last = 0 if rev else CHUNK - 1
    order = _scan_order(n_sub, rev)

    log_i = pltpu.roll(gc, 2 * N_HEADS, axis=1)
    g = _chunk_rows(a_cols, last) - a_cols + log_i
    m_loc = [jnp.max(g[j * CHUNK:(j + 1) * CHUNK], axis=0, keepdims=True) for j in range(n_sub)]
    w_in = jnp.exp(g - jnp.concatenate([jnp.broadcast_to(r, (CHUNK, LANES)) for r in m_loc], axis=0))
    m = m_ref[d]
    m_before, s_old, s_new = [None] * n_sub, [None] * n_sub, [None] * n_sub
    for j in order:
        a_end = a_cols[j * CHUNK + last:j * CHUNK + last + 1]
        m_before[j] = m
        m_new = jnp.maximum(a_end + m, m_loc[j])
        s_old[j] = jnp.exp(a_end + m - m_new)
        s_new[j] = jnp.exp(m_loc[j] - m_new)
        m = m_new
    m_ref[d] = m
    m_prev_all = jnp.concatenate([jnp.broadcast_to(r, (CHUNK, LANES)) for r in m_before], axis=0)

    for h in range(N_HEADS):
        ic = d * N_HEADS + h
        fc = 2 * N_HEADS + d * N_HEADS + h
        q_h = qk[:, h * D_HEAD:(h + 1) * D_HEAD]
        k_h = qk[:, MIX_W + h * D_HEAD:MIX_W + (h + 1) * D_HEAD]
        v_h = vb[:, h * D_HEAD:(h + 1) * D_HEAD]
        qb = q_h.astype(BF16)
        wk = w_in[:, fc:fc + 1] * k_h
        wkb = wk.astype(BF16)
        c_st = c_ref[d, h]
        n_st = n_ref[d, h]
        inter, qn = [None] * n_sub, [None] * n_sub
        for j in order:
            rows = slice(j * CHUNK, (j + 1) * CHUNK)
            if want_out:
                inter[j] = _mm(qb[rows], c_st.astype(BF16))
                qn[j] = jnp.sum(q_h[rows] * n_st, axis=1, keepdims=True)
            so = s_old[j][:, fc:fc + 1]
            sn = s_new[j][:, fc:fc + 1]
            c_st = so * c_st + sn * _mm(wkb[rows], v_h[rows], TN_DIMS)
            n_st = so * n_st + sn * jnp.sum(wk[rows], axis=0, keepdims=True)
        c_ref[d, h] = c_st
        n_ref[d, h] = n_st
        if not want_out:
            continue

        r_row = gr[ic:ic + 1, :] - a_rows[fc:fc + 1, :]
        m_prev = m_prev_all[:, fc:fc + 1]
        r_max = jnp.max(jnp.where(mask, r_row, -jnp.inf), axis=1, keepdims=True)
        mm = jnp.maximum(r_max, m_prev)
        w_intra = _mm(qb, k_h.astype(BF16), NT_DIMS) * jnp.exp(jnp.where(mask, r_row - mm, -jnp.inf))
        w_inter = jnp.exp(m_prev - mm)
        num = _mm(w_intra.astype(BF16), v_h) + w_inter * jnp.concatenate(inter, axis=0)
        den = jnp.sum(w_intra, axis=1, keepdims=True) + w_inter * jnp.concatenate(qn, axis=0)
        floor = jnp.exp(-(a_cols[:, fc:fc + 1] + mm))
        o_ref[0, :, MIX_W + h * D_HEAD:MIX_W + (h + 1) * D_HEAD] = num / jnp.maximum(jnp.abs(den), floor)


def _mix_kernel(n_cb, lbl_ref,
                hq_f, hi_f, hf_f, mv_f, qk_f, gc_f, gr_f,
                hq_b, hi_b, hf_b, mv_b, qk_b, gc_b, gr_b,
                of_ref, ob_ref, st_ref, c_ref, n_ref, m_ref):
    s = pl.program_id(1)

    @pl.when(s == 0)
    def _():
        st_ref[...] = jnp.zeros_like(st_ref)
        c_ref[...] = jnp.zeros_like(c_ref)
        n_ref[...] = jnp.zeros_like(n_ref)
        m_ref[...] = jnp.zeros_like(m_ref)

    def step(want_out):
        for d, rev, (hq, hi, hf, mv, qk, gc, gr, o_ref) in (
                (0, False, (hq_f, hi_f, hf_f, mv_f, qk_f, gc_f, gr_f, of_ref)),
                (1, True, (hq_b, hi_b, hf_b, mv_b, qk_b, gc_b, gr_b, ob_ref))):
            logits = lbl_ref[d]
            mx = jnp.max(logits, axis=0, keepdims=True)
            ex = jnp.exp(logits - mx)
            lb = ex[0:1, :] / jnp.sum(ex, axis=0, keepdims=True)
            _hgrn_dir(d, rev, want_out, hq, hi, hf, lb, st_ref, o_ref)
            _mlstm_dir(d, rev, want_out, qk, mv, gc, gr, c_ref, n_ref, m_ref, o_ref)

    @pl.when(s < n_cb)
    def _():
        step(False)

    @pl.when(s >= n_cb)
    def _():
        step(True)


def _token_mixers(p, qk, gates, gates_t, lb_logits, n_ctx):
    bsz, t_all, _ = p.shape
    n_lat = t_all - n_ctx
    n_cb = n_ctx // MIX_BLOCK
    n_blk = t_all // MIX_BLOCK

    def f_idx(s):
        return s

    def b_idx(s):
        return jnp.where(s < n_cb, n_cb - 1 - s, n_blk - 1 + n_cb - s)

    def pspec(col, idx):
        return pl.BlockSpec((1, MIX_BLOCK, MIX_W), lambda b, s: (b, idx(s), col))

    def dir_specs(idx, col_f):
        return [pspec(COL_HQ, idx), pspec(COL_HI, idx), pspec(col_f, idx), pspec(COL_MV, idx),
                pl.BlockSpec((1, MIX_BLOCK, 2 * MIX_W), lambda b, s: (b, idx(s), 0)),
                pl.BlockSpec((1, MIX_BLOCK, LANES), lambda b, s: (b, idx(s), 0)),
                pl.BlockSpec((1, N_GATES, MIX_BLOCK), lambda b, s: (b, 0, idx(s)))]

    out_f = pl.BlockSpec((1, MIX_BLOCK, 2 * MIX_W), lambda b, s: (b, jnp.maximum(s - n_cb, 0), 0))
    out_b = pl.BlockSpec((1, MIX_BLOCK, 2 * MIX_W),
                         lambda b, s: (b, jnp.minimum(n_blk - 1 - s, n_blk - 1 - n_cb), 0))
    args_dir = [p, p, p, p, qk, gates, gates_t]
    return pl.pallas_call(
        functools.partial(_mix_kernel, n_cb),
        grid=(bsz, n_blk),
        in_specs=[pl.BlockSpec((2, 2, MIX_W), lambda b, s: (0, 0, 0))]
        + dir_specs(f_idx, COL_HF0) + dir_specs(b_idx, COL_HF1),
        out_specs=[out_f, out_b],
        out_shape=[jax.ShapeDtypeStruct((bsz, n_lat, 2 * MIX_W), F32)] * 2,
        scratch_shapes=[pltpu.VMEM((2, N_HEADS, D_HEAD, D_HEAD), F32),
                        pltpu.VMEM((2, N_HEADS, D_HEAD, D_HEAD), F32),
                        pltpu.VMEM((2, N_HEADS, 1, D_HEAD), F32),
                        pltpu.VMEM((2, 1, LANES), F32)],
        compiler_params=_cparams(("arbitrary", "arbitrary")),
        name="mix",
    )(lb_logits, *args_dir, *args_dir)


def _out_kernel(of_ref, ob_ref, hg_ref, mo_ref, x_ref, mod_ref, w_ref, hgn_ref, mln_ref, lng_ref, lnb_ref,
                x1_ref, h2t_ref, h2t_bf_ref):
    o = of_ref[0] + ob_ref[0]
    hg = hg_ref[0]
    mo = mo_ref[0]
    parts = []
    for h in range(N_HEADS):
        sl = slice(h * D_HEAD, (h + 1) * D_HEAD)
        oh = o[:, sl]
        y = oh * lax.rsqrt(jnp.mean(oh * oh, axis=-1, keepdims=True) + LN_EPS) * hgn_ref[:, sl]
        parts.append(y * _silu(hg[:, sl]))
    for h in range(N_HEADS):
        sl = slice(h * D_HEAD, (h + 1) * D_HEAD)
        oh = o[:, MIX_W + h * D_HEAD:MIX_W + (h + 1) * D_HEAD]
        parts.append(_ln(oh) * mln_ref[:, sl] * _sigmoid(mo[:, sl]))
    mix = jnp.concatenate(parts, axis=-1).astype(BF16)
    y = _mm(mix, w_ref[...])
    g1 = mod_ref[0, :, 0:D_MODEL]
    sh2 = mod_ref[0, :, D_MODEL:2 * D_MODEL]
    sc2 = mod_ref[0, :, 2 * D_MODEL:3 * D_MODEL]
    x1 = _ln(DEEPNORM_ALPHA * x_ref[0] + g1 * y) * lng_ref[...] + lnb_ref[...]
    x1_ref[0] = x1
    h2t = (_ln(x1) * (1.0 + sc2) + sh2).T
    h2t_ref[...] = h2t
    h2t_bf_ref[...] = h2t.astype(BF16)


def _readout_project(o_f, o_b, p, x, mod2, w_out, hg_norm_g, ml_norm_g, ln1_g, ln1_b, n_ctx):
    bsz, n_lat, _ = x.shape
    n_tiles = n_lat // ROW_TILE
    off = n_ctx // ROW_TILE
    row = lambda a: a.reshape(1, -1)
    const2 = lambda b, t: (0, 0)
    return pl.pallas_call(
        _out_kernel,
        grid=(bsz, n_tiles),
        in_specs=[pl.BlockSpec((1, ROW_TILE, 2 * MIX_W), lambda b, t: (b, t, 0)),
                  pl.BlockSpec((1, ROW_TILE, 2 * MIX_W), lambda b, t: (b, t, 0)),
                  pl.BlockSpec((1, ROW_TILE, MIX_W), lambda b, t: (b, t + off, COL_HG)),
                  pl.BlockSpec((1, ROW_TILE, MIX_W), lambda b, t: (b, t + off, COL_MO)),
                  pl.BlockSpec((1, ROW_TILE, D_MODEL), lambda b, t: (b, t, 0)),
                  pl.BlockSpec((1, 1, 3 * D_MODEL), lambda b, t: (b, 0, 0)),
                  pl.BlockSpec((2 * MIX_W, D_MODEL), const2),
                  pl.BlockSpec((1, MIX_W), const2),
                  pl.BlockSpec((1, MIX_W), const2),
                  pl.BlockSpec((1, D_MODEL), const2),
                  pl.BlockSpec((1, D_MODEL), const2)],
        out_specs=[pl.BlockSpec((1, ROW_TILE, D_MODEL), lambda b, t: (b, t, 0)),
                   pl.BlockSpec((D_MODEL, ROW_TILE), lambda b, t: (0, b * n_tiles + t)),
                   pl.BlockSpec((D_MODEL, ROW_TILE), lambda b, t: (0, b * n_tiles + t))],
        out_shape=[jax.ShapeDtypeStruct((bsz, n_lat, D_MODEL), F32),
                   jax.ShapeDtypeStruct((D_MODEL, bsz * n_lat), F32),
                   jax.ShapeDtypeStruct((D_MODEL, bsz * n_lat), BF16)],
        compiler_params=_cparams(("arbitrary", "arbitrary")),
        name="readout",
    )(o_f, o_b, p, p, x, mod2, w_out.astype(BF16), row(hg_norm_g), row(ml_norm_g), row(ln1_g), row(ln1_b))


def _extract_exact(s):
    ridx = lax.broadcasted_iota(jnp.int32, s.shape, 0).astype(F32)
    rank = jnp.full(s.shape, float(PEER_TOPK), F32)
    vals = []
    for it in range(PEER_TOPK):
        m = jnp.max(s, axis=0, keepdims=True)
        first = jnp.min(jnp.where(s == m, ridx, float(s.shape[0])), axis=0, keepdims=True)
        hit = ridx == first
        rank = jnp.where(hit, float(it), rank)
        s = jnp.where(hit, -jnp.inf, s)
        vals.append(m)
    return rank, vals


def _extract_fast(s):
    rank = jnp.full(s.shape, float(PEER_TOPK), F32)
    vals = []
    for it in range(PEER_TOPK):
        m = jnp.max(s, axis=0, keepdims=True)
        hit = s == m
        rank = jnp.where(hit, float(it), rank)
        s = jnp.where(hit, -jnp.inf, s)
        vals.append(m)
    return rank, vals


def _count_true(mask):
    return jnp.sum(jnp.where(mask, 1.0, 0.0), axis=0, keepdims=True)


def _cand_grid(v1, v2):
    v1a = jnp.concatenate(v1, axis=0)
    v2a = jnp.concatenate(v2, axis=0)
    row = lax.broadcasted_iota(jnp.int32, (8, v1a.shape[1]), 0)
    blocks = [v1a[0:1] + v2a]
    for a in range(1, 8):
        nb = PEER_TOPK // (a + 1)
        blk = v1a[a:a + 1] + v2a[0:8]
        blocks.append(blk if nb >= 8 else jnp.where(row < nb, blk, -jnp.inf))
    blocks.append(v1a[8:16] + v2a[0:1])
    return jnp.concatenate(blocks, axis=0)


def _cand_counts(chosen):
    cnt = [_count_true(chosen[0:PEER_TOPK])]
    for a in range(1, 8):
        cnt.append(_count_true(chosen[PEER_TOPK + 8 * (a - 1):PEER_TOPK + 8 * a]))
    base = PEER_TOPK + 8 * 7
    for r in range(8):
        cnt.append(jnp.where(chosen[base + r:base + r + 1], 1.0, 0.0))
    return cnt


def _sel_emit(hd, s1, s2, is_rank1, rank2, v1, v2, cand, chosen, r2_ref, ci_ref, a_ref, b_ref):
    top = v1[0] + v2[0]
    z = jnp.sum(jnp.where(chosen, jnp.exp(cand - top), 0.0), axis=0, keepdims=True)
    ci = jnp.zeros_like(s1)
    for a, cnt_a in enumerate(_cand_counts(chosen)):
        ci = jnp.where(is_rank1(a), cnt_a, ci)
    r2_ref[hd] = rank2.astype(BF16)
    ci_ref[hd] = ci
    a_ref[hd] = jnp.exp(s1 - v1[0])
    b_ref[hd] = (jnp.exp(s2 - v2[0]) / z).astype(BF16)


def _sel_kernel(h_ref, wq_hi_ref, wq_lo_ref, k_hi_ref, k_lo_ref, r2_ref, ci_ref, a_ref, b_ref):
    h2t = h_ref[...]
    h_hi, h_lo = _split(h2t, 2)
    qt = (_mm(wq_hi_ref[...], h_hi) + (_mm(wq_lo_ref[...], h_hi) + _mm(wq_hi_ref[...], h_lo)))
    half = PEER_DQ // 2
    for hd in range(PEER_HEADS):
        s = []
        for p_ in range(2):
            qh = qt[hd * PEER_DQ + p_ * half: hd * PEER_DQ + (p_ + 1) * half, :]
            q_hi, q_lo = _split(qh, 2)
            k_hi = k_hi_ref[hd, p_]
            k_lo = k_lo_ref[hd, p_]
            s.append(_mm(k_hi, q_hi) + (_mm(k_lo, q_hi) + _mm(k_hi, q_lo)))
        s1, s2 = s

        _, v1 = _extract_fast(s1)
        rank2, v2 = _extract_fast(s2)
        cand = _cand_grid(v1, v2)
        _, cv = _extract_fast(cand)
        chosen = cand >= cv[PEER_TOPK - 1]
        k = float(PEER_TOPK)
        tied = ((_count_true(s1 >= v1[PEER_TOPK - 1]) != k) | (_count_true(rank2 < k) != k)
                | (_count_true(chosen) != k))
        any_tied = jnp.max(jnp.where(tied, 1.0, 0.0)) > 0.0
        _sel_emit(hd, s1, s2, lambda a: s1 == v1[a], rank2, v1, v2, cand, chosen, r2_ref, ci_ref, a_ref, b_ref)

        @pl.when(any_tied)
        def _():
            rank1, v1 = _extract_exact(s1)
            rank2, v2 = _extract_exact(s2)
            cand = _cand_grid(v1, v2)
            crank, _ = _extract_exact(cand)
            _sel_emit(hd, s1, s2, lambda a: rank1 == float(a), rank2, v1, v2, cand, crank < k,
                      r2_ref, ci_ref, a_ref, b_ref)


def _peer_select(h2t, peer_wq, peer_keys):
    n_tok = h2t.shape[1]
    wqt = peer_wq.T
    wq_hi = wqt.astype(BF16)
    wq_lo = (wqt - wq_hi.astype(F32)).astype(BF16)
    k_hi = peer_keys.astype(BF16)
    k_lo = (peer_keys - k_hi.astype(F32)).astype(BF16)
    dq_all = PEER_HEADS * PEER_DQ
    out_spec = pl.BlockSpec((PEER_HEADS, PEER_NKEYS, SEL_TILE), lambda t: (0, 0, t))
    shape = (PEER_HEADS, PEER_NKEYS, n_tok)
    return pl.pallas_call(
        _sel_kernel,
        grid=(n_tok // SEL_TILE,),
        in_specs=[pl.BlockSpec((D_MODEL, SEL_TILE), lambda t: (0, t)),
                  pl.BlockSpec((dq_all, D_MODEL), lambda t: (0, 0)),
                  pl.BlockSpec((dq_all, D_MODEL), lambda t: (0, 0)),
                  pl.BlockSpec((PEER_HEADS, 2, PEER_NKEYS, PEER_DQ // 2), lambda t: (0, 0, 0, 0)),
                  pl.BlockSpec((PEER_HEADS, 2, PEER_NKEYS, PEER_DQ // 2), lambda t: (0, 0, 0, 0))],
        out_specs=[out_spec] * 4,
        out_shape=[jax.ShapeDtypeStruct(shape, BF16), jax.ShapeDtypeStruct(shape, F32),
                   jax.ShapeDtypeStruct(shape, F32), jax.ShapeDtypeStruct(shape, BF16)],
        compiler_params=_cparams(("arbitrary",)),
        name="peersel",
    )(h2t, wq_hi, wq_lo, k_hi, k_lo)


def _rows_bf16(row):
    tile = jnp.broadcast_to(row, (16, row.shape[1])).astype(BF16)
    return jnp.concatenate([tile] * (PEER_NKEYS // 16), axis=0)


def _peer_kernel(n_e, h_ref, r2_ref, b_ref, ci_ref, a_ref, u_ref, vt_ref, x1_ref, g2_ref, lng_ref, lnb_ref,
                 o_ref, acc_ref, pre_ref, wa_ref):
    e = pl.program_id(1)

    @pl.when(e == 0)
    def _():
        acc_ref[...] = jnp.zeros_like(acc_ref)

    pre_ref[...] = _mm(u_ref[...], h_ref[...])
    zero = jnp.zeros((), BF16)
    for ii in range(PEER_ET // PEER_NKEYS):
        w = None
        for hd in range(PEER_HEADS):
            cnt = _rows_bf16(ci_ref[hd, ii:ii + 1, :])
            fac = _rows_bf16(a_ref[hd, ii:ii + 1, :])
            term = jnp.where(r2_ref[hd] < cnt, b_ref[hd] * fac, zero)
            w = term if w is None else w + term
        rows = pl.ds(ii * PEER_NKEYS, PEER_NKEYS)
        wa_ref[rows, :] = w * _gelu(pre_ref[rows, :].astype(BF16))
    acc_ref[...] += _mm(vt_ref[...], wa_ref[...])

    @pl.when(e == n_e - 1)
    def _():
        y = acc_ref[...].T
        o_ref[0] = _ln(DEEPNORM_ALPHA * x1_ref[0] + g2_ref[0] * y) * lng_ref[...] + lnb_ref[...]


def _peer_dense(h2t_bf, r2, ci, a_fac, b_fac, u_bf, vt_bf, x1, g2, ln2_g, ln2_b):
    bsz, n_lat, _ = x1.shape
    n_tok = bsz * n_lat
    n_t = n_tok // PEER_TT
    per_b = n_lat // PEER_TT
    n_e = u_bf.shape[0] // PEER_ET
    per = PEER_ET // PEER_NKEYS
    assert per % 8 == 0, "whole sublane tiles of half-1 keys per expert step"
    tok_spec = pl.BlockSpec((PEER_HEADS, PEER_NKEYS, PEER_TT), lambda t, e: (0, 0, t))
    key_spec = pl.BlockSpec((PEER_HEADS, per, PEER_TT), lambda t, e: (0, e, t))
    const2 = lambda t, e: (0, 0)
    return pl.pallas_call(
        functools.partial(_peer_kernel, n_e),
        grid=(n_t, n_e),
        in_specs=[pl.BlockSpec((D_MODEL, PEER_TT), lambda t, e: (0, t)),
                  tok_spec, tok_spec, key_spec, key_spec,
                  pl.BlockSpec((PEER_ET, D_MODEL), lambda t, e: (e, 0)),
                  pl.BlockSpec((D_MODEL, PEER_ET), lambda t, e: (0, e)),
                  pl.BlockSpec((1, PEER_TT, D_MODEL), lambda t, e: (t // per_b, t % per_b, 0)),
                  pl.BlockSpec((1, 1, D_MODEL), lambda t, e: (t // per_b, 0, 0)),
                  pl.BlockSpec((1, D_MODEL), const2),
                  pl.BlockSpec((1, D_MODEL), const2)],
        out_specs=pl.BlockSpec((1, PEER_TT, D_MODEL), lambda t, e: (t // per_b, t % per_b, 0)),
        out_shape=jax.ShapeDtypeStruct((bsz, n_lat, D_MODEL), F32),
        scratch_shapes=[pltpu.VMEM((D_MODEL, PEER_TT), F32),
                        pltpu.VMEM((PEER_ET, PEER_TT), F32),
                        pltpu.VMEM((PEER_ET, PEER_TT), BF16)],
        compiler_params=_cparams(("arbitrary", "arbitrary")),
        name="peer",
    )(h2t_bf, r2, b_fac, ci, a_fac, u_bf, vt_bf, x1, g2, ln2_g.reshape(1, -1), ln2_b.reshape(1, -1))


def kernel(x, c, ctx, c_ctx, w_mod, b_mod, w_in, hg_lb_logits, hg_norm_g, ml_conv_w, ml_conv_b, ml_gate_b,
           ml_norm_g, w_out, ln1_g, ln1_b, peer_wq, peer_keys, peer_u, peer_v, ln2_g, ln2_b):
    bsz, n_lat, _ = x.shape
    n_ctx = ctx.shape[1]
    assert n_ctx % ROW_TILE == 0 and n_lat % ROW_TILE == 0 and ROW_TILE % CHUNK == 0
    assert n_ctx % MIX_BLOCK == 0 and n_lat % MIX_BLOCK == 0 and MIX_BLOCK % CHUNK == 0
    assert w_mod.shape[0] == 1, "single-layer kernel"
    lyr = 0

    n_rows = -(-(bsz + 1) // 8) * 8
    c_all = jnp.zeros((n_rows, D_MODEL), F32).at[:bsz].set(c).at[bsz].set(c_ctx)
    mod = _modulation(c_all, w_mod[lyr], b_mod[lyr])
    mod_l, mod_c = mod[:bsz], mod[bsz]
    n_tiles_ctx = n_ctx // ROW_TILE
    n_tiles = (n_ctx + n_lat) // ROW_TILE
    mod1_l = mod_l[:, None, :2 * D_MODEL]
    mod1_c = jnp.broadcast_to(mod_c[None, None, :2 * D_MODEL], (bsz, 1, 2 * D_MODEL))
    mod1 = jnp.concatenate([jnp.repeat(mod1_c, n_tiles_ctx, axis=1),
                            jnp.repeat(mod1_l, n_tiles - n_tiles_ctx, axis=1)], axis=1)[:, :, None, :]
    mod2 = mod_l[:, None, 2 * D_MODEL:5 * D_MODEL]
    g2 = mod_l[:, None, 5 * D_MODEL:6 * D_MODEL]

    w_main = w_in[lyr][:, :D_MAIN].astype(BF16)
    w_gates = w_in[lyr][:, D_MAIN:]
    p, gates, gates_t = _input_projection(ctx, x, mod1, w_main, w_gates, ml_gate_b[lyr])

    qk = _qk_conv(p, ml_conv_w[lyr], ml_conv_b[lyr], n_ctx)
    o_f, o_b = _token_mixers(p, qk, gates, gates_t, hg_lb_logits[:, lyr:lyr + 2], n_ctx)
    x1, h2t, h2t_bf = _readout_project(o_f, o_b, p, x, mod2, w_out[lyr], hg_norm_g[lyr], ml_norm_g[lyr],
                                       ln1_g[lyr], ln1_b[lyr], n_ctx)
    r2, ci, a_fac, b_fac = _peer_select(h2t, peer_wq[lyr], peer_keys[lyr])
    u_bf = peer_u[lyr].astype(BF16)
    vt_bf = peer_v[lyr].astype(BF16).T
    return _peer_dense(h2t_bf, r2, ci, a_fac, b_fac, u_bf, vt_bf, x1, g2, ln2_g[lyr], ln2_b[lyr])
```

```python
import functools
import math

import jax
import jax.numpy as jnp
from jax import lax
from jax.experimental import pallas as pl
from jax.experimental.pallas import tpu as pltpu

F32 = jnp.float32
BF16 = jnp.bfloat16

D_MODEL = 1024
CHUNK = 64
GRID_W = 64
N_HEADS = 4
D_HEAD = 128
MIX_W = N_HEADS * D_HEAD
N_GATES = 4 * N_HEADS
D_MAIN = 9 * MIX_W
PEER_HEADS = 8
PEER_NKEYS = 128
PEER_TOPK = 16
PEER_DQ = 256
LN_EPS = 1e-6
DEEPNORM_ALPHA = 2.0 ** 0.25
LANES = 128
VMEM_LIMIT = 56 * 1024 * 1024

COL_HQ, COL_HI, COL_HG, COL_HF0, COL_HF1, COL_MQ, COL_MK, COL_MV, COL_MO = range(9)

ROW_TILE = 256
SEL_TILE = 256
PEER_TT = 512
PEER_ET = 2048

NT_DIMS = (((1,), (1,)), ((), ()))
TN_DIMS = (((0,), (0,)), ((), ()))


def _cparams(sem):
    return pltpu.CompilerParams(dimension_semantics=sem, vmem_limit_bytes=VMEM_LIMIT)


def _split(a, n):
    parts = []
    r = a
    for k in range(n):
        p = r.astype(BF16)
        parts.append(p)
        if k + 1 < n:
            r = r - p.astype(F32)
    return parts


def _mm(a, b, dims=None):
    if dims is None:
        return jnp.dot(a, b, preferred_element_type=F32)
    return lax.dot_general(a, b, dims, preferred_element_type=F32)


def _mm_bf16(a, b, dims=None):
    return _mm(a.astype(BF16), b.astype(BF16), dims)


def _mm_x3(a, b, dims=None):
    a_hi, a_lo = _split(a, 2)
    b_hi, b_lo = _split(b, 2)
    return _mm(a_hi, b_hi, dims) + (_mm(a_hi, b_lo, dims) + _mm(a_lo, b_hi, dims))


def _mm_x3_pre(a, b_hi, b_lo, dims=None):
    a_hi, a_lo = _split(a, 2)
    return _mm(a_hi, b_hi, dims) + (_mm(a_hi, b_lo, dims) + _mm(a_lo, b_hi, dims))


def _mm_exact_left(m_bf16, x):
    x0, x1, x2 = _split(x, 3)
    return _mm(m_bf16, x0) + (_mm(m_bf16, x1) + _mm(m_bf16, x2))


def _mm_exact_right(x, m_bf16):
    x0, x1, x2 = _split(x, 3)
    return _mm(x0, m_bf16) + (_mm(x1, m_bf16) + _mm(x2, m_bf16))


def _sigmoid(x):
    return 1.0 / (1.0 + jnp.exp(-x))


def _silu(x):
    return x * _sigmoid(x)


def _log_sigmoid(x):
    return jnp.minimum(x, 0.0) - jnp.log(1.0 + jnp.exp(-jnp.abs(x)))


def _ln(x):
    mu = jnp.mean(x, axis=-1, keepdims=True)
    xc = x - mu
    var = jnp.mean(xc * xc, axis=-1, keepdims=True)
    return xc * lax.rsqrt(var + LN_EPS)


def _gelu(x):
    return 0.5 * x * (1.0 + lax.erf(x * (2.0 ** -0.5)))


def _mod_kernel(c_ref, w_ref, b_ref, o_ref):
    s = _silu(c_ref[...])
    o_ref[...] = _mm_x3(s, w_ref[...]) + b_ref[...]


def _modulation(c_all, w_mod, b_mod):
    n = c_all.shape[0]
    d_out = w_mod.shape[1]
    blk = 1024
    return pl.pallas_call(
        _mod_kernel,
        grid=(d_out // blk,),
        in_specs=[pl.BlockSpec((n, D_MODEL), lambda j: (0, 0)),
                  pl.BlockSpec((D_MODEL, blk), lambda j: (0, j)),
                  pl.BlockSpec((1, blk), lambda j: (0, j))],
        out_specs=pl.BlockSpec((n, blk), lambda j: (0, j)),
        out_shape=jax.ShapeDtypeStruct((n, d_out), F32),
        compiler_params=_cparams(("arbitrary",)),
        name="mod",
    )(c_all, w_mod, b_mod.reshape(1, d_out))


def _inproj_kernel(n_ctx_tiles, ctx_ref, x_ref, mod_ref, w_ref, wg_hi_ref, wg_lo_ref, gb_ref,
                   p_ref, g_ref, gt_ref):
    x = jnp.where(pl.program_id(1) < n_ctx_tiles, ctx_ref[0], x_ref[0])
    shift = mod_ref[0, 0, :, :D_MODEL]
    scale = mod_ref[0, 0, :, D_MODEL:]
    h = _ln(x) * (1.0 + scale) + shift
    p_ref[0] = _mm(h.astype(BF16), w_ref[...])
    h_hi, h_lo = _split(h, 2)
    g = _mm(h_hi, wg_hi_ref[...]) + (_mm(h_hi, wg_lo_ref[...]) + _mm(h_lo, wg_hi_ref[...])) + gb_ref[...]
    g_ref[0] = g
    gt_ref[0] = g.T[:N_GATES, :]


def _input_projection(ctx, x, mod1, w_main, wg, gate_b):
    bsz, n_ctx, _ = ctx.shape
    n_ctx_tiles = n_ctx // ROW_TILE
    t_all = n_ctx + x.shape[1]
    n_tiles = t_all // ROW_TILE
    wg_pad = jnp.zeros((D_MODEL, LANES), F32).at[:, :N_GATES].set(wg)
    wg_hi = wg_pad.astype(BF16)
    wg_lo = (wg_pad - wg_hi.astype(F32)).astype(BF16)
    gb = jnp.zeros((1, LANES), F32).at[0, :N_GATES].set(gate_b)
    const2 = lambda b, t: (0, 0)
    return pl.pallas_call(
        functools.partial(_inproj_kernel, n_ctx_tiles),
        grid=(bsz, n_tiles),
        in_specs=[pl.BlockSpec((1, ROW_TILE, D_MODEL), lambda b, t: (b, jnp.minimum(t, n_ctx_tiles - 1), 0)),
                  pl.BlockSpec((1, ROW_TILE, D_MODEL), lambda b, t: (b, jnp.maximum(t - n_ctx_tiles, 0), 0)),
                  pl.BlockSpec((1, 1, 1, 2 * D_MODEL), lambda b, t: (b, t, 0, 0)),
                  pl.BlockSpec((D_MODEL, D_MAIN), const2),
                  pl.BlockSpec((D_MODEL, LANES), const2),
                  pl.BlockSpec((D_MODEL, LANES), const2),
                  pl.BlockSpec((1, LANES), const2)],
        out_specs=[pl.BlockSpec((1, ROW_TILE, D_MAIN), lambda b, t: (b, t, 0)),
                   pl.BlockSpec((1, ROW_TILE, LANES), lambda b, t: (b, t, 0)),
                   pl.BlockSpec((1, N_GATES, ROW_TILE), lambda b, t: (b, 0, t))],
        out_shape=[jax.ShapeDtypeStruct((bsz, t_all, D_MAIN), F32),
                   jax.ShapeDtypeStruct((bsz, t_all, LANES), F32),
                   jax.ShapeDtypeStruct((bsz, N_GATES, t_all), F32)],
        compiler_params=_cparams(("arbitrary", "arbitrary")),
        name="inproj",
    )(ctx, x, mod1, w_main, wg_hi, wg_lo, gb)


CONV_PAD = 72
CONV_CB = 256


def _conv_kernel(n_ctx, n_lat, p_ref, w_ref, b_ref, o_ref, pad_c, pad_l):
    cb = pl.program_id(1)
    w = w_ref[...]
    bias = b_ref[...]
    scale = jnp.where(cb >= MIX_W // CONV_CB, D_HEAD ** -0.5, 1.0).astype(F32)

    def finish(y):
        return _silu(y + bias) * scale

    pad_c[...] = jnp.zeros_like(pad_c)
    pad_c[pl.ds(CONV_PAD, n_ctx), :] = p_ref[0, pl.ds(0, n_ctx), :]
    y = jnp.zeros((n_ctx, CONV_CB), F32)
    for dc in range(3):
        y = y + pad_c[pl.ds(CONV_PAD + dc - 1, n_ctx), :] * w[3 + dc:4 + dc, :]
    o_ref[0, pl.ds(0, n_ctx), :] = finish(y)

    x = p_ref[0, pl.ds(n_ctx, n_lat), :]
    col = lax.broadcasted_iota(jnp.int32, (n_lat, CONV_CB), 0) & (GRID_W - 1)
    pad_l[...] = jnp.zeros_like(pad_l)
    pad_l[0, pl.ds(CONV_PAD + 1, n_lat), :] = jnp.where(col == GRID_W - 1, 0.0, x)
    pad_l[1, pl.ds(CONV_PAD, n_lat), :] = x
    pad_l[2, pl.ds(CONV_PAD - 1, n_lat), :] = jnp.where(col == 0, 0.0, x)
    y = jnp.zeros((n_lat, CONV_CB), F32)
    for dr in range(3):
        for dc in range(3):
            rows = pl.ds(CONV_PAD + (dr - 1) * GRID_W, n_lat)
            y = y + pad_l[dc, rows, :] * w[3 * dr + dc:3 * dr + dc + 1, :]
    o_ref[0, pl.ds(n_ctx, n_lat), :] = finish(y)


def _qk_conv(p, conv_w, conv_b, n_ctx):
    bsz, t_all, _ = p.shape
    n_lat = t_all - n_ctx
    n_cb = 2 * MIX_W // CONV_CB
    first = COL_MQ * MIX_W // CONV_CB
    w9 = conv_w.reshape(9, 2 * MIX_W)
    return pl.pallas_call(
        functools.partial(_conv_kernel, n_ctx, n_lat),
        grid=(bsz, n_cb),
        in_specs=[pl.BlockSpec((1, t_all, CONV_CB), lambda b, c: (b, 0, first + c)),
                  pl.BlockSpec((9, CONV_CB), lambda b, c: (0, c)),
                  pl.BlockSpec((1, CONV_CB), lambda b, c: (0, c))],
        out_specs=pl.BlockSpec((1, t_all, CONV_CB), lambda b, c: (b, 0, c)),
        out_shape=jax.ShapeDtypeStruct((bsz, t_all, 2 * MIX_W), F32),
        scratch_shapes=[pltpu.VMEM((n_ctx + 2 * CONV_PAD, CONV_CB), F32),
                        pltpu.VMEM((3, n_lat + 2 * CONV_PAD, CONV_CB), F32)],
        compiler_params=_cparams(("arbitrary", "arbitrary")),
        name="qkconv",
    )(p, w9, conv_b.reshape(1, 2 * MIX_W))


MIX_BLOCK = 256
MIX_BATCH = 2


def _block_mask(t, rev):
    r = lax.broadcasted_iota(jnp.int32, (t, t), 0)
    c = lax.broadcasted_iota(jnp.int32, (t, t), 1)
    shift = CHUNK.bit_length() - 1
    same = (r >> shift) == (c >> shift)
    return jnp.logical_and(same, (c >= r) if rev else (c <= r))


def _chunk_rows(x, idx):
    return jnp.concatenate(
        [jnp.broadcast_to(x[j * CHUNK + idx:j * CHUNK + idx + 1], (CHUNK, x.shape[1]))
         for j in range(x.shape[0] // CHUNK)], axis=0)


def _scan_order(n, rev):
    return range(n - 1, -1, -1) if rev else range(n)


def _hgrn_dir(d, rev, want_out, q_ref, v_ref, z_ref, lb, st_ref, o_ref):
    vb = v_ref[0].astype(BF16)
    sig = _sigmoid(z_ref[0])
    log_f = jnp.log(lb + (1.0 - lb) * sig)
    kk = (1.0 - lb) * (1.0 - sig)
    t = vb.shape[0]
    n_sub = t // CHUNK
    mask = _block_mask(t, rev)
    tri = jnp.where(mask, 1.0, 0.0).astype(BF16)
    b = _mm_exact_left(tri, log_f)
    last = 0 if rev else CHUNK - 1
    mid = CHUNK // 2 - 1 if rev else CHUNK // 2
    kdec = (kk * jnp.exp(_chunk_rows(b, last) - b)).astype(BF16)
    if want_out:
        q = q_ref[0]
        b_mid = _chunk_rows(b, mid)
        qd = (q * jnp.exp(b - b_mid)).astype(BF16)
        kd = (kk * jnp.exp(b_mid - b)).astype(BF16)
        qe = (q * jnp.exp(b)).astype(BF16)
    for h in range(N_HEADS):
        sl = slice(h * D_HEAD, (h + 1) * D_HEAD)
        st = st_ref[d, h]
        inter = [None] * n_sub
        for j in _scan_order(n_sub, rev):
            rows = slice(j * CHUNK, (j + 1) * CHUNK)
            if want_out:
                inter[j] = _mm(qe[rows, sl], st.astype(BF16), NT_DIMS)
            e_last = jnp.exp(b[j * CHUNK + last:j * CHUNK + last + 1, sl])
            st = st * e_last + _mm(vb[rows, sl], kdec[rows, sl], TN_DIMS)
        st_ref[d, h] = st
        if want_out:
            sc = _mm(qd[:, sl], kd[:, sl], NT_DIMS)
            o = _mm(jnp.where(mask, sc, 0.0).astype(BF16), vb[:, sl])
            o_ref[0, :, sl] = (o + jnp.concatenate(inter, axis=0)).astype(o_ref.dtype)


def _mlstm_dir(d, rev, want_out, qk_ref, v_ref, gc_ref, gr_ref, c_ref, n_ref, m_ref, o_ref):
    qk = qk_ref[0]
    vb = v_ref[0].astype(BF16)
    gc = gc_ref[0]
    gr = gr_ref[0]
    t = gc.shape[0]
    n_sub = t // CHUNK
    mask = _block_mask(t, rev)
    tri = jnp.where(mask, 1.0, 0.0).astype(BF16)
    a_cols = _mm_exact_left(tri, _log_sigmoid(gc))
    if want_out:
        tri_t = jnp.where(_block_mask(t, not rev), 1.0, 0.0).astype(BF16)
        a_rows = _mm_exact_right(_log_sigmoid(gr), tri_t)
    last = 0 if rev else CHUNK - 1
    order = _scan_order(n_sub, rev)

    log_i = pltpu.roll(gc, 2 * N_HEADS, axis=1)
    g = _chunk_rows(a_cols, last) - a_cols + log_i
    m_loc = [jnp.max(g[j * CHUNK:(j + 1) * CHUNK], axis=0, keepdims=True) for j in range(n_sub)]
    w_in = jnp.exp(g - jnp.concatenate([jnp.broadcast_to(r, (CHUNK, LANES)) for r in m_loc], axis=0))
    m = m_ref[d]
    m_before, s_old, s_new = [None] * n_sub, [None] * n_sub, [None] * n_sub
    for j in order:
        a_end = a_cols[j * CHUNK + last:j * CHUNK + last + 1]
        m_before[j] = m
        m_new = jnp.maximum(a_end + m, m_loc[j])
        s_old[j] = jnp.exp(a_end + m - m_new)
        s_new[j] = jnp.exp(m_loc[j] - m_new)
        m = m_new
    m_ref[d] = m
    m_prev_all = jnp.concatenate([jnp.broadcast_to(r, (CHUNK, LANES)) for r in m_before], axis=0)

    for h in range(N_HEADS):
        ic = d * N_HEADS + h
        fc = 2 * N_HEADS + d * N_HEADS + h
        q_h = qk[:, h * D_HEAD:(h + 1) * D_HEAD]
        k_h = qk[:, MIX_W + h * D_HEAD:MIX_W + (h + 1) * D_HEAD]
        v_h = vb[:, h * D_HEAD:(h + 1) * D_HEAD]
        qb = q_h.astype(BF16)
        wk = w_in[:, fc:fc + 1] * k_h
        wkb = wk.astype(BF16)
        c_st = c_ref[d, h]
        n_st = n_ref[d, h]
        inter, qn = [None] * n_sub, [None] * n_sub
        for j in order:
            rows = slice(j * CHUNK, (j + 1) * CHUNK)
            if want_out:
                inter[j] = _mm(qb[rows], c_st.astype(BF16))
                qn[j] = jnp.sum(q_h[rows] * n_st, axis=1, keepdims=True)
            so = s_old[j][:, fc:fc + 1]
            sn = s_new[j][:, fc:fc + 1]
            c_st = so * c_st + sn * _mm(wkb[rows], v_h[rows], TN_DIMS)
            n_st = so * n_st + sn * jnp.sum(wk[rows], axis=0, keepdims=True)
        c_ref[d, h] = c_st
        n_ref[d, h] = n_st
        if not want_out:
            continue

        r_row = gr[ic:ic + 1, :] - a_rows[fc:fc + 1, :]
        m_prev = m_prev_all[:, fc:fc + 1]
        r_max = jnp.max(jnp.where(mask, r_row, -jnp.inf), axis=1, keepdims=True)
        mm = jnp.maximum(r_max, m_prev)
        w_intra = _mm(qb, k_h.astype(BF16), NT_DIMS) * jnp.exp(jnp.where(mask, r_row - mm, -jnp.inf))
        w_inter = jnp.exp(m_prev - mm)
        num = _mm(w_intra.astype(BF16), v_h) + w_inter * jnp.concatenate(inter, axis=0)
        den = jnp.sum(w_intra, axis=1, keepdims=True) + w_inter * jnp.concatenate(qn, axis=0)
        floor = jnp.exp(-(a_cols[:, fc:fc + 1] + mm))
        o_ref[0, :, MIX_W + h * D_HEAD:MIX_W + (h + 1) * D_HEAD] = (
            num / jnp.maximum(jnp.abs(den), floor)).astype(o_ref.dtype)


def _mix_kernel(n_cb, lbl_ref,
                hq_f, hi_f, hf_f, mv_f, qk_f, gc_f, gr_f,
                hq_b, hi_b, hf_b, mv_b, qk_b, gc_b, gr_b,
                of_ref, ob_ref, st_ref, c_ref, n_ref, m_ref):
    s = pl.program_id(1)

    @pl.when(s == 0)
    def _():
        st_ref[...] = jnp.zeros_like(st_ref)
        c_ref[...] = jnp.zeros_like(c_ref)
        n_ref[...] = jnp.zeros_like(n_ref)
        m_ref[...] = jnp.zeros_like(m_ref)

    def step(want_out):
        for d, rev, dir_refs in (
                (0, False, (hq_f, hi_f, hf_f, mv_f, qk_f, gc_f, gr_f, of_ref)),
                (1, True, (hq_b, hi_b, hf_b, mv_b, qk_b, gc_b, gr_b, ob_ref))):
            logits = lbl_ref[d]
            mx = jnp.max(logits, axis=0, keepdims=True)
            ex = jnp.exp(logits - mx)
            lb = ex[0:1, :] / jnp.sum(ex, axis=0, keepdims=True)
            for bb in range(MIX_BATCH):
                hq, hi, hf, mv, qk, gc, gr, o_ref = (r.at[pl.ds(bb, 1)] for r in dir_refs)
                _hgrn_dir(d, rev, want_out, hq, hi, hf, lb, st_ref.at[bb], o_ref)
                _mlstm_dir(d, rev, want_out, qk, mv, gc, gr, c_ref.at[bb], n_ref.at[bb], m_ref.at[bb], o_ref)

    @pl.when(s < n_cb)
    def _():
        step(False)

    @pl.when(s >= n_cb)
    def _():
        step(True)


def _token_mixers(p, qk, gates, gates_t, lb_logits, n_ctx):
    bsz, t_all, _ = p.shape
    n_lat = t_all - n_ctx
    n_cb = n_ctx // MIX_BLOCK
    n_blk = t_all // MIX_BLOCK

    def f_idx(s):
        return s

    def b_idx(s):
        return jnp.where(s < n_cb, n_cb - 1 - s, n_blk - 1 + n_cb - s)

    nb = MIX_BATCH
    assert bsz % nb == 0

    def pspec(col, idx):
        return pl.BlockSpec((nb, MIX_BLOCK, MIX_W), lambda b, s: (b, idx(s), col))

    def dir_specs(idx, col_f):
        return [pspec(COL_HQ, idx), pspec(COL_HI, idx), pspec(col_f, idx), pspec(COL_MV, idx),
                pl.BlockSpec((nb, MIX_BLOCK, 2 * MIX_W), lambda b, s: (b, idx(s), 0)),
                pl.BlockSpec((nb, MIX_BLOCK, LANES), lambda b, s: (b, idx(s), 0)),
                pl.BlockSpec((nb, N_GATES, MIX_BLOCK), lambda b, s: (b, 0, idx(s)))]

    out_f = pl.BlockSpec((nb, MIX_BLOCK, 2 * MIX_W), lambda b, s: (b, jnp.maximum(s - n_cb, 0), 0))
    out_b = pl.BlockSpec((nb, MIX_BLOCK, 2 * MIX_W),
                         lambda b, s: (b, jnp.minimum(n_blk - 1 - s, n_blk - 1 - n_cb), 0))
    args_dir = [p, p, p, p, qk, gates, gates_t]
    return pl.pallas_call(
        functools.partial(_mix_kernel, n_cb),
        grid=(bsz // nb, n_blk),
        in_specs=[pl.BlockSpec((2, 2, MIX_W), lambda b, s: (0, 0, 0))]
        + dir_specs(f_idx, COL_HF0) + dir_specs(b_idx, COL_HF1),
        out_specs=[out_f, out_b],
        out_shape=[jax.ShapeDtypeStruct((bsz, n_lat, 2 * MIX_W), BF16)] * 2,
        scratch_shapes=[pltpu.VMEM((nb, 2, N_HEADS, D_HEAD, D_HEAD), F32),
                        pltpu.VMEM((nb, 2, N_HEADS, D_HEAD, D_HEAD), F32),
                        pltpu.VMEM((nb, 2, N_HEADS, 1, D_HEAD), F32),
                        pltpu.VMEM((nb, 2, 1, LANES), F32)],
        compiler_params=_cparams(("arbitrary", "arbitrary")),
        name="mix",
    )(lb_logits, *args_dir, *args_dir)


def _out_kernel(of_ref, ob_ref, hg_ref, mo_ref, x_ref, mod_ref, w_ref, hgn_ref, mln_ref, lng_ref, lnb_ref,
                x1_ref, h2t_ref, h2t_bf_ref):
    o = of_ref[0].astype(F32) + ob_ref[0].astype(F32)
    hg = hg_ref[0]
    mo = mo_ref[0]
    parts = []
    for h in range(N_HEADS):
        sl = slice(h * D_HEAD, (h + 1) * D_HEAD)
        oh = o[:, sl]
        y = oh * lax.rsqrt(jnp.mean(oh * oh, axis=-1, keepdims=True) + LN_EPS) * hgn_ref[:, sl]
        parts.append(y * _silu(hg[:, sl]))
    for h in range(N_HEADS):
        sl = slice(h * D_HEAD, (h + 1) * D_HEAD)
        oh = o[:, MIX_W + h * D_HEAD:MIX_W + (h + 1) * D_HEAD]
        parts.append(_ln(oh) * mln_ref[:, sl] * _sigmoid(mo[:, sl]))
    mix = jnp.concatenate(parts, axis=-1).astype(BF16)
    y = _mm(mix, w_ref[...])
    g1 = mod_ref[0, :, 0:D_MODEL]
    sh2 = mod_ref[0, :, D_MODEL:2 * D_MODEL]
    sc2 = mod_ref[0, :, 2 * D_MODEL:3 * D_MODEL]
    x1 = _ln(DEEPNORM_ALPHA * x_ref[0] + g1 * y) * lng_ref[...] + lnb_ref[...]
    x1_ref[0] = x1
    h2t = (_ln(x1) * (1.0 + sc2) + sh2).T
    h2t_ref[...] = h2t
    h2t_bf_ref[...] = h2t.astype(BF16)


def _readout_project(o_f, o_b, p, x, mod2, w_out, hg_norm_g, ml_norm_g, ln1_g, ln1_b, n_ctx):
    bsz, n_lat, _ = x.shape
    n_tiles = n_lat // ROW_TILE
    off = n_ctx // ROW_TILE
    row = lambda a: a.reshape(1, -1)
    const2 = lambda b, t: (0, 0)
    return pl.pallas_call(
        _out_kernel,
        grid=(bsz, n_tiles),
        in_specs=[pl.BlockSpec((1, ROW_TILE, 2 * MIX_W), lambda b, t: (b, t, 0)),
                  pl.BlockSpec((1, ROW_TILE, 2 * MIX_W), lambda b, t: (b, t, 0)),
                  pl.BlockSpec((1, ROW_TILE, MIX_W), lambda b, t: (b, t + off, COL_HG)),
                  pl.BlockSpec((1, ROW_TILE, MIX_W), lambda b, t: (b, t + off, COL_MO)),
                  pl.BlockSpec((1, ROW_TILE, D_MODEL), lambda b, t: (b, t, 0)),
                  pl.BlockSpec((1, 1, 3 * D_MODEL), lambda b, t: (b, 0, 0)),
                  pl.BlockSpec((2 * MIX_W, D_MODEL), const2),
                  pl.BlockSpec((1, MIX_W), const2),
                  pl.BlockSpec((1, MIX_W), const2),
                  pl.BlockSpec((1, D_MODEL), const2),
                  pl.BlockSpec((1, D_MODEL), const2)],
        out_specs=[pl.BlockSpec((1, ROW_TILE, D_MODEL), lambda b, t: (b, t, 0)),
                   pl.BlockSpec((D_MODEL, ROW_TILE), lambda b, t: (0, b * n_tiles + t)),
                   pl.BlockSpec((D_MODEL, ROW_TILE), lambda b, t: (0, b * n_tiles + t))],
        out_shape=[jax.ShapeDtypeStruct((bsz, n_lat, D_MODEL), F32),
                   jax.ShapeDtypeStruct((D_MODEL, bsz * n_lat), F32),
                   jax.ShapeDtypeStruct((D_MODEL, bsz * n_lat), BF16)],
        compiler_params=_cparams(("arbitrary", "arbitrary")),
        name="readout",
    )(o_f, o_b, p, p, x, mod2, w_out.astype(BF16), row(hg_norm_g), row(ml_norm_g), row(ln1_g), row(ln1_b))


def _extract_exact(s):
    ridx = lax.broadcasted_iota(jnp.int32, s.shape, 0).astype(F32)
    rank = jnp.full(s.shape, float(PEER_TOPK), F32)
    vals = []
    for it in range(PEER_TOPK):
        m = jnp.max(s, axis=0, keepdims=True)
        first = jnp.min(jnp.where(s == m, ridx, float(s.shape[0])), axis=0, keepdims=True)
        hit = ridx == first
        rank = jnp.where(hit, float(it), rank)
        s = jnp.where(hit, -jnp.inf, s)
        vals.append(m)
    return rank, vals


def _extract_fast(s):
    rank = jnp.full(s.shape, float(PEER_TOPK), F32)
    vals = []
    for it in range(PEER_TOPK):
        m = jnp.max(s, axis=0, keepdims=True)
        hit = s == m
        rank = jnp.where(hit, float(it), rank)
        s = jnp.where(hit, -jnp.inf, s)
        vals.append(m)
    return rank, vals


def _count_true(mask):
    return jnp.sum(jnp.where(mask, 1.0, 0.0), axis=0, keepdims=True)


def _cand_grid(v1, v2):
    v1a = jnp.concatenate(v1, axis=0)
    v2a = jnp.concatenate(v2, axis=0)
    row = lax.broadcasted_iota(jnp.int32, (8, v1a.shape[1]), 0)
    blocks = [v1a[0:1] + v2a]
    for a in range(1, 8):
        nb = PEER_TOPK // (a + 1)
        blk = v1a[a:a + 1] + v2a[0:8]
        blocks.append(blk if nb >= 8 else jnp.where(row < nb, blk, -jnp.inf))
    blocks.append(v1a[8:16] + v2a[0:1])
    return jnp.concatenate(blocks, axis=0)


def _cand_counts(chosen):
    cnt = [_count_true(chosen[0:PEER_TOPK])]
    for a in range(1, 8):
        cnt.append(_count_true(chosen[PEER_TOPK + 8 * (a - 1):PEER_TOPK + 8 * a]))
    base = PEER_TOPK + 8 * 7
    for r in range(8):
        cnt.append(jnp.where(chosen[base + r:base + r + 1], 1.0, 0.0))
    return cnt


def _sel_emit(hd, s1, s2, is_rank1, rank2, v1, v2, cand, chosen, r2_ref, ci_ref, a_ref, b_ref):
    top = v1[0] + v2[0]
    z = jnp.sum(jnp.where(chosen, jnp.exp(cand - top), 0.0), axis=0, keepdims=True)
    ci = jnp.zeros_like(s1)
    for a, cnt_a in enumerate(_cand_counts(chosen)):
        ci = jnp.where(is_rank1(a), cnt_a, ci)
    r2_ref[hd] = rank2.astype(BF16)
    ci_ref[hd] = ci
    a_ref[hd] = jnp.exp(s1 - v1[0])
    b_ref[hd] = (jnp.exp(s2 - v2[0]) / z).astype(BF16)


def _sel_kernel(h_ref, wq_hi_ref, wq_lo_ref, k_hi_ref, k_lo_ref, r2_ref, ci_ref, a_ref, b_ref):
    h2t = h_ref[...]
    h_hi, h_lo = _split(h2t, 2)
    qt = (_mm(wq_hi_ref[...], h_hi) + (_mm(wq_lo_ref[...], h_hi) + _mm(wq_hi_ref[...], h_lo)))
    half = PEER_DQ // 2
    for hd in range(PEER_HEADS):
        s = []
        for p_ in range(2):
            qh = qt[hd * PEER_DQ + p_ * half: hd * PEER_DQ + (p_ + 1) * half, :]
            q_hi, q_lo = _split(qh, 2)
            k_hi = k_hi_ref[hd, p_]
            k_lo = k_lo_ref[hd, p_]
            s.append(_mm(k_hi, q_hi) + (_mm(k_lo, q_hi) + _mm(k_hi, q_lo)))
        s1, s2 = s

        _, v1 = _extract_fast(s1)
        rank2, v2 = _extract_fast(s2)
        cand = _cand_grid(v1, v2)
        _, cv = _extract_fast(cand)
        chosen = cand >= cv[PEER_TOPK - 1]
        k = float(PEER_TOPK)
        tied = ((_count_true(s1 >= v1[PEER_TOPK - 1]) != k) | (_count_true(rank2 < k) != k)
                | (_count_true(chosen) != k))
        any_tied = jnp.max(jnp.where(tied, 1.0, 0.0)) > 0.0
        _sel_emit(hd, s1, s2, lambda a: s1 == v1[a], rank2, v1, v2, cand, chosen, r2_ref, ci_ref, a_ref, b_ref)

        @pl.when(any_tied)
        def _():
            rank1, v1 = _extract_exact(s1)
            rank2, v2 = _extract_exact(s2)
            cand = _cand_grid(v1, v2)
            crank, _ = _extract_exact(cand)
            _sel_emit(hd, s1, s2, lambda a: rank1 == float(a), rank2, v1, v2, cand, crank < k,
                      r2_ref, ci_ref, a_ref, b_ref)


def _peer_select(h2t, peer_wq, peer_keys):
    n_tok = h2t.shape[1]
    wqt = peer_wq.T
    wq_hi = wqt.astype(BF16)
    wq_lo = (wqt - wq_hi.astype(F32)).astype(BF16)
    k_hi = peer_keys.astype(BF16)
    k_lo = (peer_keys - k_hi.astype(F32)).astype(BF16)
    dq_all = PEER_HEADS * PEER_DQ
    out_spec = pl.BlockSpec((PEER_HEADS, PEER_NKEYS, SEL_TILE), lambda t: (0, 0, t))
    shape = (PEER_HEADS, PEER_NKEYS, n_tok)
    return pl.pallas_call(
        _sel_kernel,
        grid=(n_tok // SEL_TILE,),
        in_specs=[pl.BlockSpec((D_MODEL, SEL_TILE), lambda t: (0, t)),
                  pl.BlockSpec((dq_all, D_MODEL), lambda t: (0, 0)),
                  pl.BlockSpec((dq_all, D_MODEL), lambda t: (0, 0)),
                  pl.BlockSpec((PEER_HEADS, 2, PEER_NKEYS, PEER_DQ // 2), lambda t: (0, 0, 0, 0)),
                  pl.BlockSpec((PEER_HEADS, 2, PEER_NKEYS, PEER_DQ // 2), lambda t: (0, 0, 0, 0))],
        out_specs=[out_spec] * 4,
        out_shape=[jax.ShapeDtypeStruct(shape, BF16), jax.ShapeDtypeStruct(shape, F32),
                   jax.ShapeDtypeStruct(shape, F32), jax.ShapeDtypeStruct(shape, BF16)],
        compiler_params=_cparams(("arbitrary",)),
        name="peersel",
    )(h2t, wq_hi, wq_lo, k_hi, k_lo)


def _rows_bf16(row):
    tile = jnp.broadcast_to(row, (16, row.shape[1])).astype(BF16)
    return jnp.concatenate([tile] * (PEER_NKEYS // 16), axis=0)


def _peer_kernel(n_e, h_ref, r2_ref, b_ref, ci_ref, a_ref, u_ref, vt_ref, x1_ref, g2_ref, lng_ref, lnb_ref,
                 o_ref, acc_ref, pre_ref, wa_ref):
    e = pl.program_id(1)

    @pl.when(e == 0)
    def _():
        acc_ref[...] = jnp.zeros_like(acc_ref)

    pre_ref[...] = _mm(u_ref[...], h_ref[...])
    zero = jnp.zeros((), BF16)
    for ii in range(PEER_ET // PEER_NKEYS):
        w = None
        for hd in range(PEER_HEADS):
            cnt = _rows_bf16(ci_ref[hd, ii:ii + 1, :])
            fac = _rows_bf16(a_ref[hd, ii:ii + 1, :])
            term = jnp.where(r2_ref[hd] < cnt, b_ref[hd] * fac, zero)
            w = term if w is None else w + term
        rows = pl.ds(ii * PEER_NKEYS, PEER_NKEYS)
        wa_ref[rows, :] = w * _gelu(pre_ref[rows, :].astype(BF16))
    acc_ref[...] += _mm(vt_ref[...], wa_ref[...])

    @pl.when(e == n_e - 1)
    def _():
        y = acc_ref[...].T
        o_ref[0] = _ln(DEEPNORM_ALPHA * x1_ref[0] + g2_ref[0] * y) * lng_ref[...] + lnb_ref[...]


def _peer_dense(h2t_bf, r2, ci, a_fac, b_fac, u_bf, vt_bf, x1, g2, ln2_g, ln2_b):
    bsz, n_lat, _ = x1.shape
    n_tok = bsz * n_lat
    n_t = n_tok // PEER_TT
    per_b = n_lat // PEER_TT
    n_e = u_bf.shape[0] // PEER_ET
    per = PEER_ET // PEER_NKEYS
    assert per % 8 == 0, "whole sublane tiles of half-1 keys per expert step"
    tok_spec = pl.BlockSpec((PEER_HEADS, PEER_NKEYS, PEER_TT), lambda t, e: (0, 0, t))
    key_spec = pl.BlockSpec((PEER_HEADS, per, PEER_TT), lambda t, e: (0, e, t))
    const2 = lambda t, e: (0, 0)
    return pl.pallas_call(
        functools.partial(_peer_kernel, n_e),
        grid=(n_t, n_e),
        in_specs=[pl.BlockSpec((D_MODEL, PEER_TT), lambda t, e: (0, t)),
                  tok_spec, tok_spec, key_spec, key_spec,
                  pl.BlockSpec((PEER_ET, D_MODEL), lambda t, e: (e, 0)),
                  pl.BlockSpec((D_MODEL, PEER_ET), lambda t, e: (0, e)),
                  pl.BlockSpec((1, PEER_TT, D_MODEL), lambda t, e: (t // per_b, t % per_b, 0)),
                  pl.BlockSpec((1, 1, D_MODEL), lambda t, e: (t // per_b, 0, 0)),
                  pl.BlockSpec((1, D_MODEL), const2),
                  pl.BlockSpec((1, D_MODEL), const2)],
        out_specs=pl.BlockSpec((1, PEER_TT, D_MODEL), lambda t, e: (t // per_b, t % per_b, 0)),
        out_shape=jax.ShapeDtypeStruct((bsz, n_lat, D_MODEL), F32),
        scratch_shapes=[pltpu.VMEM((D_MODEL, PEER_TT), F32),
                        pltpu.VMEM((PEER_ET, PEER_TT), F32),
                        pltpu.VMEM((PEER_ET, PEER_TT), BF16)],
        compiler_params=_cparams(("arbitrary", "arbitrary")),
        name="peer",
    )(h2t_bf, r2, b_fac, ci, a_fac, u_bf, vt_bf, x1, g2, ln2_g.reshape(1, -1), ln2_b.reshape(1, -1))


def kernel(x, c, ctx, c_ctx, w_mod, b_mod, w_in, hg_lb_logits, hg_norm_g, ml_conv_w, ml_conv_b, ml_gate_b,
           ml_norm_g, w_out, ln1_g, ln1_b, peer_wq, peer_keys, peer_u, peer_v, ln2_g, ln2_b):
    bsz, n_lat, _ = x.shape
    n_ctx = ctx.shape[1]
    assert n_ctx % ROW_TILE == 0 and n_lat % ROW_TILE == 0 and ROW_TILE % CHUNK == 0
    assert n_ctx % MIX_BLOCK == 0 and n_lat % MIX_BLOCK == 0 and MIX_BLOCK % CHUNK == 0
    assert w_mod.shape[0] == 1, "single-layer kernel"
    lyr = 0

    n_rows = -(-(bsz + 1) // 8) * 8
    c_all = jnp.zeros((n_rows, D_MODEL), F32).at[:bsz].set(c).at[bsz].set(c_ctx)
    mod = _modulation(c_all, w_mod[lyr], b_mod[lyr])
    mod_l, mod_c = mod[:bsz], mod[bsz]
    n_tiles_ctx = n_ctx // ROW_TILE
    n_tiles = (n_ctx + n_lat) // ROW_TILE
    mod1_l = mod_l[:, None, :2 * D_MODEL]
    mod1_c = jnp.broadcast_to(mod_c[None, None, :2 * D_MODEL], (bsz, 1, 2 * D_MODEL))
    mod1 = jnp.concatenate([jnp.repeat(mod1_c, n_tiles_ctx, axis=1),
                            jnp.repeat(mod1_l, n_tiles - n_tiles_ctx, axis=1)], axis=1)[:, :, None, :]
    mod2 = mod_l[:, None, 2 * D_MODEL:5 * D_MODEL]
    g2 = mod_l[:, None, 5 * D_MODEL:6 * D_MODEL]

    w_main = w_in[lyr][:, :D_MAIN].astype(BF16)
    w_gates = w_in[lyr][:, D_MAIN:]
    p, gates, gates_t = _input_projection(ctx, x, mod1, w_main, w_gates, ml_gate_b[lyr])

    qk = _qk_conv(p, ml_conv_w[lyr], ml_conv_b[lyr], n_ctx)
    o_f, o_b = _token_mixers(p, qk, gates, gates_t, hg_lb_logits[:, lyr:lyr + 2], n_ctx)
    x1, h2t, h2t_bf = _readout_project(o_f, o_b, p, x, mod2, w_out[lyr], hg_norm_g[lyr], ml_norm_g[lyr],
                                       ln1_g[lyr], ln1_b[lyr], n_ctx)
    r2, ci, a_fac, b_fac = _peer_select(h2t, peer_wq[lyr], peer_keys[lyr])
    u_bf = peer_u[lyr].astype(BF16)
    vt_bf = peer_v[lyr].astype(BF16).T
    return _peer_dense(h2t_bf, r2, ci, a_fac, b_fac, u_bf, vt_bf, x1, g2, ln2_g[lyr], ln2_b[lyr])
```

```python
import functools
import math

import jax
import jax.numpy as jnp
from jax import lax
from jax.experimental import pallas as pl
from jax.experimental.pallas import tpu as pltpu

F32 = jnp.float32
BF16 = jnp.bfloat16

D_MODEL = 1024
CHUNK = 64
GRID_W = 64
N_HEADS = 4
D_HEAD = 128
MIX_W = N_HEADS * D_HEAD
N_GATES = 4 * N_HEADS
D_MAIN = 9 * MIX_W
PEER_HEADS = 8
PEER_NKEYS = 128
PEER_TOPK = 16
PEER_DQ = 256
LN_EPS = 1e-6
DEEPNORM_ALPHA = 2.0 ** 0.25
LANES = 128
VMEM_LIMIT = 56 * 1024 * 1024

COL_HQ, COL_HI, COL_HG, COL_HF0, COL_HF1, COL_MQ, COL_MK, COL_MV, COL_MO = range(9)

ROW_TILE = 256
SEL_TILE = 256
PEER_TT = 512
PEER_ET = 2048

NT_DIMS = (((1,), (1,)), ((), ()))
TN_DIMS = (((0,), (0,)), ((), ()))


def _cparams(sem):
    return pltpu.CompilerParams(dimension_semantics=sem, vmem_limit_bytes=VMEM_LIMIT)


def _split(a, n):
    parts = []
    r = a
    for k in range(n):
        p = r.astype(BF16)
        parts.append(p)
        if k + 1 < n:
            r = r - p.astype(F32)
    return parts


def _mm(a, b, dims=None):
    if dims is None:
        return jnp.dot(a, b, preferred_element_type=F32)
    return lax.dot_general(a, b, dims, preferred_element_type=F32)


def _mm_bf16(a, b, dims=None):
    return _mm(a.astype(BF16), b.astype(BF16), dims)


def _mm_x3(a, b, dims=None):
    a_hi, a_lo = _split(a, 2)
    b_hi, b_lo = _split(b, 2)
    return _mm(a_hi, b_hi, dims) + (_mm(a_hi, b_lo, dims) + _mm(a_lo, b_hi, dims))


def _mm_x3_pre(a, b_hi, b_lo, dims=None):
    a_hi, a_lo = _split(a, 2)
    return _mm(a_hi, b_hi, dims) + (_mm(a_hi, b_lo, dims) + _mm(a_lo, b_hi, dims))


def _mm_exact_left(m_bf16, x):
    x0, x1 = _split(x, 2)
    return _mm(m_bf16, x0) + _mm(m_bf16, x1)


def _mm_exact_right(x, m_bf16):
    x0, x1 = _split(x, 2)
    return _mm(x0, m_bf16) + _mm(x1, m_bf16)


def _sigmoid(x):
    return 1.0 / (1.0 + jnp.exp(-x))


def _silu(x):
    return x * _sigmoid(x)


def _log_sigmoid(x):
    return jnp.minimum(x, 0.0) - jnp.log(1.0 + jnp.exp(-jnp.abs(x)))


def _ln(x):
    mu = jnp.mean(x, axis=-1, keepdims=True)
    xc = x - mu
    var = jnp.mean(xc * xc, axis=-1, keepdims=True)
    return xc * lax.rsqrt(var + LN_EPS)


def _gelu(x):
    return 0.5 * x * (1.0 + lax.erf(x * (2.0 ** -0.5)))


def _mod_kernel(c_ref, w_ref, b_ref, o_ref):
    s = _silu(c_ref[...])
    o_ref[...] = _mm_x3(s, w_ref[...]) + b_ref[...]


def _modulation(c_all, w_mod, b_mod):
    n = c_all.shape[0]
    d_out = w_mod.shape[1]
    blk = 1024
    return pl.pallas_call(
        _mod_kernel,
        grid=(d_out // blk,),
        in_specs=[pl.BlockSpec((n, D_MODEL), lambda j: (0, 0)),
                  pl.BlockSpec((D_MODEL, blk), lambda j: (0, j)),
                  pl.BlockSpec((1, blk), lambda j: (0, j))],
        out_specs=pl.BlockSpec((n, blk), lambda j: (0, j)),
        out_shape=jax.ShapeDtypeStruct((n, d_out), F32),
        compiler_params=_cparams(("arbitrary",)),
        name="mod",
    )(c_all, w_mod, b_mod.reshape(1, d_out))


def _inproj_kernel(n_ctx_tiles, ctx_ref, x_ref, mod_ref, w_ref, wg_hi_ref, wg_lo_ref, gb_ref,
                   p_ref, g_ref, gt_ref):
    x = jnp.where(pl.program_id(1) < n_ctx_tiles, ctx_ref[0], x_ref[0])
    shift = mod_ref[0, 0, :, :D_MODEL]
    scale = mod_ref[0, 0, :, D_MODEL:]
    h = _ln(x) * (1.0 + scale) + shift
    p_ref[0] = _mm(h.astype(BF16), w_ref[...])
    h_hi, h_lo = _split(h, 2)
    g = _mm(h_hi, wg_hi_ref[...]) + (_mm(h_hi, wg_lo_ref[...]) + _mm(h_lo, wg_hi_ref[...])) + gb_ref[...]
    g_ref[0] = g
    gt_ref[0] = g.T[:N_GATES, :]


def _input_projection(ctx, x, mod1, w_main, wg, gate_b):
    bsz, n_ctx, _ = ctx.shape
    n_ctx_tiles = n_ctx // ROW_TILE
    t_all = n_ctx + x.shape[1]
    n_tiles = t_all // ROW_TILE
    wg_pad = jnp.zeros((D_MODEL, LANES), F32).at[:, :N_GATES].set(wg)
    wg_hi = wg_pad.astype(BF16)
    wg_lo = (wg_pad - wg_hi.astype(F32)).astype(BF16)
    gb = jnp.zeros((1, LANES), F32).at[0, :N_GATES].set(gate_b)
    const2 = lambda b, t: (0, 0)
    return pl.pallas_call(
        functools.partial(_inproj_kernel, n_ctx_tiles),
        grid=(bsz, n_tiles),
        in_specs=[pl.BlockSpec((1, ROW_TILE, D_MODEL), lambda b, t: (b, jnp.minimum(t, n_ctx_tiles - 1), 0)),
                  pl.BlockSpec((1, ROW_TILE, D_MODEL), lambda b, t: (b, jnp.maximum(t - n_ctx_tiles, 0), 0)),
                  pl.BlockSpec((1, 1, 1, 2 * D_MODEL), lambda b, t: (b, t, 0, 0)),
                  pl.BlockSpec((D_MODEL, D_MAIN), const2),
                  pl.BlockSpec((D_MODEL, LANES), const2),
                  pl.BlockSpec((D_MODEL, LANES), const2),
                  pl.BlockSpec((1, LANES), const2)],
        out_specs=[pl.BlockSpec((1, ROW_TILE, D_MAIN), lambda b, t: (b, t, 0)),
                   pl.BlockSpec((1, ROW_TILE, LANES), lambda b, t: (b, t, 0)),
                   pl.BlockSpec((1, N_GATES, ROW_TILE), lambda b, t: (b, 0, t))],
        out_shape=[jax.ShapeDtypeStruct((bsz, t_all, D_MAIN), F32),
                   jax.ShapeDtypeStruct((bsz, t_all, LANES), F32),
                   jax.ShapeDtypeStruct((bsz, N_GATES, t_all), F32)],
        compiler_params=_cparams(("arbitrary", "arbitrary")),
        name="inproj",
    )(ctx, x, mod1, w_main, wg_hi, wg_lo, gb)


CONV_PAD = 72
CONV_CB = 256


def _conv_kernel(n_ctx, n_lat, p_ref, w_ref, b_ref, o_ref, pad_c, pad_l):
    cb = pl.program_id(1)
    w = w_ref[...]
    bias = b_ref[...]
    scale = jnp.where(cb >= MIX_W // CONV_CB, D_HEAD ** -0.5, 1.0).astype(F32)

    def finish(y):
        return _silu(y + bias) * scale

    pad_c[...] = jnp.zeros_like(pad_c)
    pad_c[pl.ds(CONV_PAD, n_ctx), :] = p_ref[0, pl.ds(0, n_ctx), :]
    y = jnp.zeros((n_ctx, CONV_CB), F32)
    for dc in range(3):
        y = y + pad_c[pl.ds(CONV_PAD + dc - 1, n_ctx), :] * w[3 + dc:4 + dc, :]
    o_ref[0, pl.ds(0, n_ctx), :] = finish(y)

    x = p_ref[0, pl.ds(n_ctx, n_lat), :]
    col = lax.broadcasted_iota(jnp.int32, (n_lat, CONV_CB), 0) & (GRID_W - 1)
    pad_l[...] = jnp.zeros_like(pad_l)
    pad_l[0, pl.ds(CONV_PAD + 1, n_lat), :] = jnp.where(col == GRID_W - 1, 0.0, x)
    pad_l[1, pl.ds(CONV_PAD, n_lat), :] = x
    pad_l[2, pl.ds(CONV_PAD - 1, n_lat), :] = jnp.where(col == 0, 0.0, x)
    y = jnp.zeros((n_lat, CONV_CB), F32)
    for dr in range(3):
        for dc in range(3):
            rows = pl.ds(CONV_PAD + (dr - 1) * GRID_W, n_lat)
            y = y + pad_l[dc, rows, :] * w[3 * dr + dc:3 * dr + dc + 1, :]
    o_ref[0, pl.ds(n_ctx, n_lat), :] = finish(y)


def _qk_conv(p, conv_w, conv_b, n_ctx):
    bsz, t_all, _ = p.shape
    n_lat = t_all - n_ctx
    n_cb = 2 * MIX_W // CONV_CB
    first = COL_MQ * MIX_W // CONV_CB
    w9 = conv_w.reshape(9, 2 * MIX_W)
    return pl.pallas_call(
        functools.partial(_conv_kernel, n_ctx, n_lat),
        grid=(bsz, n_cb),
        in_specs=[pl.BlockSpec((1, t_all, CONV_CB), lambda b, c: (b, 0, first + c)),
                  pl.BlockSpec((9, CONV_CB), lambda b, c: (0, c)),
                  pl.BlockSpec((1, CONV_CB), lambda b, c: (0, c))],
        out_specs=pl.BlockSpec((1, t_all, CONV_CB), lambda b, c: (b, 0, c)),
        out_shape=jax.ShapeDtypeStruct((bsz, t_all, 2 * MIX_W), F32),
        scratch_shapes=[pltpu.VMEM((n_ctx + 2 * CONV_PAD, CONV_CB), F32),
                        pltpu.VMEM((3, n_lat + 2 * CONV_PAD, CONV_CB), F32)],
        compiler_params=_cparams(("arbitrary", "arbitrary")),
        name="qkconv",
    )(p, w9, conv_b.reshape(1, 2 * MIX_W))


MIX_BLOCK = 256
MIX_BATCH = 2


def _block_mask(t, rev):
    r = lax.broadcasted_iota(jnp.int32, (t, t), 0)
    c = lax.broadcasted_iota(jnp.int32, (t, t), 1)
    shift = CHUNK.bit_length() - 1
    same = (r >> shift) == (c >> shift)
    return jnp.logical_and(same, (c >= r) if rev else (c <= r))


def _chunk_rows(x, idx):
    return jnp.concatenate(
        [jnp.broadcast_to(x[j * CHUNK + idx:j * CHUNK + idx + 1], (CHUNK, x.shape[1]))
         for j in range(x.shape[0] // CHUNK)], axis=0)


def _scan_order(n, rev):
    return range(n - 1, -1, -1) if rev else range(n)


def _chunk_expand(x):
    n = x.shape[0] // CHUNK
    chunk = lax.broadcasted_iota(jnp.int32, x.shape, 0) >> (CHUNK.bit_length() - 1)
    return jnp.concatenate([jnp.where(chunk == j, x, jnp.zeros((), x.dtype)) for j in range(n)], axis=1)


def _diag_blocks(y, w):
    n = y.shape[0] // CHUNK
    return jnp.concatenate([y[j * CHUNK:(j + 1) * CHUNK, j * w:(j + 1) * w] for j in range(n)], axis=0)


def _hgrn_dir(d, rev, want_out, q_ref, v_ref, z_ref, lb, st_ref, o_ref):
    vb = v_ref[0].astype(BF16)
    sig = _sigmoid(z_ref[0])
    log_f = jnp.log(lb + (1.0 - lb) * sig)
    kk = (1.0 - lb) * (1.0 - sig)
    t = vb.shape[0]
    n_sub = t // CHUNK
    mask = _block_mask(t, rev)
    tri = jnp.where(mask, 1.0, 0.0).astype(BF16)
    b = _mm_exact_left(tri, log_f)
    last = 0 if rev else CHUNK - 1
    mid = CHUNK // 2 - 1 if rev else CHUNK // 2
    kdec = (kk * jnp.exp(_chunk_rows(b, last) - b)).astype(BF16)
    if want_out:
        q = q_ref[0]
        b_mid = _chunk_rows(b, mid)
        qd = (q * jnp.exp(b - b_mid)).astype(BF16)
        kd = (kk * jnp.exp(b_mid - b)).astype(BF16)
        qe = (q * jnp.exp(b)).astype(BF16)
    for h in range(N_HEADS):
        sl = slice(h * D_HEAD, (h + 1) * D_HEAD)
        upd = _mm(vb[:, sl], _chunk_expand(kdec[:, sl]), TN_DIMS)
        st = st_ref[d, h]
        before = [None] * n_sub
        for j in _scan_order(n_sub, rev):
            before[j] = st
            e_last = jnp.exp(b[j * CHUNK + last:j * CHUNK + last + 1, sl])
            st = st * e_last + upd[:, j * D_HEAD:(j + 1) * D_HEAD]
        st_ref[d, h] = st
        if want_out:
            sc = _mm(qd[:, sl], kd[:, sl], NT_DIMS)
            o = _mm(jnp.where(mask, sc, 0.0).astype(BF16), vb[:, sl])
            stacked = jnp.concatenate([s_.astype(BF16) for s_ in before], axis=0)
            inter = _diag_blocks(_mm(qe[:, sl], stacked, NT_DIMS), D_HEAD)
            o_ref[0, :, sl] = (o + inter).astype(o_ref.dtype)


def _mlstm_dir(d, rev, want_out, qk_ref, v_ref, gc_ref, gr_ref, c_ref, n_ref, m_ref, o_ref):
    qk = qk_ref[0]
    vb = v_ref[0].astype(BF16)
    gc = gc_ref[0]
    gr = gr_ref[0]
    t = gc.shape[0]
    n_sub = t // CHUNK
    mask = _block_mask(t, rev)
    tri = jnp.where(mask, 1.0, 0.0).astype(BF16)
    a_cols = _mm_exact_left(tri, _log_sigmoid(gc))
    if want_out:
        tri_t = jnp.where(_block_mask(t, not rev), 1.0, 0.0).astype(BF16)
        a_rows = _mm_exact_right(_log_sigmoid(gr), tri_t)
    last = 0 if rev else CHUNK - 1
    order = _scan_order(n_sub, rev)

    log_i = pltpu.roll(gc, 2 * N_HEADS, axis=1)
    g = _chunk_rows(a_cols, last) - a_cols + log_i
    m_loc = [jnp.max(g[j * CHUNK:(j + 1) * CHUNK], axis=0, keepdims=True) for j in range(n_sub)]
    w_in = jnp.exp(g - jnp.concatenate([jnp.broadcast_to(r, (CHUNK, LANES)) for r in m_loc], axis=0))
    m = m_ref[d]
    m_before, s_old, s_new = [None] * n_sub, [None] * n_sub, [None] * n_sub
    for j in order:
        a_end = a_cols[j * CHUNK + last:j * CHUNK + last + 1]
        m_before[j] = m
        m_new = jnp.maximum(a_end + m, m_loc[j])
        s_old[j] = jnp.exp(a_end + m - m_new)
        s_new[j] = jnp.exp(m_loc[j] - m_new)
        m = m_new
    m_ref[d] = m
    m_prev_all = jnp.concatenate([jnp.broadcast_to(r, (CHUNK, LANES)) for r in m_before], axis=0)

    for h in range(N_HEADS):
        ic = d * N_HEADS + h
        fc = 2 * N_HEADS + d * N_HEADS + h
        q_h = qk[:, h * D_HEAD:(h + 1) * D_HEAD]
        k_h = qk[:, MIX_W + h * D_HEAD:MIX_W + (h + 1) * D_HEAD]
        v_h = vb[:, h * D_HEAD:(h + 1) * D_HEAD]
        qb = q_h.astype(BF16)
        wk = w_in[:, fc:fc + 1] * k_h
        wkb = wk.astype(BF16)
        upd = _mm(wkb, _chunk_expand(v_h), TN_DIMS)
        c_st = c_ref[d, h]
        n_st = n_ref[d, h]
        c_before, qn = [None] * n_sub, [None] * n_sub
        for j in order:
            rows = slice(j * CHUNK, (j + 1) * CHUNK)
            c_before[j] = c_st
            if want_out:
                qn[j] = jnp.sum(q_h[rows] * n_st, axis=1, keepdims=True)
            so = s_old[j][:, fc:fc + 1]
            sn = s_new[j][:, fc:fc + 1]
            c_st = so * c_st + sn * upd[:, j * D_HEAD:(j + 1) * D_HEAD]
            n_st = so * n_st + sn * jnp.sum(wk[rows], axis=0, keepdims=True)
        c_ref[d, h] = c_st
        n_ref[d, h] = n_st
        if not want_out:
            continue
        side = jnp.concatenate([c_.astype(BF16) for c_ in c_before], axis=1)
        inter = _diag_blocks(_mm(qb, side), D_HEAD)

        r_row = gr[ic:ic + 1, :] - a_rows[fc:fc + 1, :]
        m_prev = m_prev_all[:, fc:fc + 1]
        r_max = jnp.max(jnp.where(mask, r_row, -jnp.inf), axis=1, keepdims=True)
        mm = jnp.maximum(r_max, m_prev)
        w_intra = _mm(qb, k_h.astype(BF16), NT_DIMS) * jnp.exp(jnp.where(mask, r_row - mm, -jnp.inf))
        w_inter = jnp.exp(m_prev - mm)
        num = _mm(w_intra.astype(BF16), v_h) + w_inter * inter
        den = jnp.sum(w_intra, axis=1, keepdims=True) + w_inter * jnp.concatenate(qn, axis=0)
        floor = jnp.exp(-(a_cols[:, fc:fc + 1] + mm))
        o_ref[0, :, MIX_W + h * D_HEAD:MIX_W + (h + 1) * D_HEAD] = (
            num / jnp.maximum(jnp.abs(den), floor)).astype(o_ref.dtype)


def _mix_kernel(n_cb, lbl_ref,
                hq_f, hi_f, hf_f, mv_f, qk_f, gc_f, gr_f,
                hq_b, hi_b, hf_b, mv_b, qk_b, gc_b, gr_b,
                of_ref, ob_ref, st_ref, c_ref, n_ref, m_ref):
    s = pl.program_id(1)

    @pl.when(s == 0)
    def _():
        st_ref[...] = jnp.zeros_like(st_ref)
        c_ref[...] = jnp.zeros_like(c_ref)
        n_ref[...] = jnp.zeros_like(n_ref)
        m_ref[...] = jnp.zeros_like(m_ref)

    def step(want_out):
        for d, rev, dir_refs in (
                (0, False, (hq_f, hi_f, hf_f, mv_f, qk_f, gc_f, gr_f, of_ref)),
                (1, True, (hq_b, hi_b, hf_b, mv_b, qk_b, gc_b, gr_b, ob_ref))):
            logits = lbl_ref[d]
            mx = jnp.max(logits, axis=0, keepdims=True)
            ex = jnp.exp(logits - mx)
            lb = ex[0:1, :] / jnp.sum(ex, axis=0, keepdims=True)
            for bb in range(MIX_BATCH):
                hq, hi, hf, mv, qk, gc, gr, o_ref = (r.at[pl.ds(bb, 1)] for r in dir_refs)
                _hgrn_dir(d, rev, want_out, hq, hi, hf, lb, st_ref.at[bb], o_ref)
                _mlstm_dir(d, rev, want_out, qk, mv, gc, gr, c_ref.at[bb], n_ref.at[bb], m_ref.at[bb], o_ref)

    @pl.when(s < n_cb)
    def _():
        step(False)

    @pl.when(s >= n_cb)
    def _():
        step(True)


def _token_mixers(p, qk, gates, gates_t, lb_logits, n_ctx):
    bsz, t_all, _ = p.shape
    n_lat = t_all - n_ctx
    n_cb = n_ctx // MIX_BLOCK
    n_blk = t_all // MIX_BLOCK

    def f_idx(s):
        return s

    def b_idx(s):
        return jnp.where(s < n_cb, n_cb - 1 - s, n_blk - 1 + n_cb - s)

    nb = MIX_BATCH
    assert bsz % nb == 0

    def pspec(col, idx):
        return pl.BlockSpec((nb, MIX_BLOCK, MIX_W), lambda b, s: (b, idx(s), col))

    def dir_specs(idx, col_f):
        return [pspec(COL_HQ, idx), pspec(COL_HI, idx), pspec(col_f, idx), pspec(COL_MV, idx),
                pl.BlockSpec((nb, MIX_BLOCK, 2 * MIX_W), lambda b, s: (b, idx(s), 0)),
                pl.BlockSpec((nb, MIX_BLOCK, LANES), lambda b, s: (b, idx(s), 0)),
                pl.BlockSpec((nb, N_GATES, MIX_BLOCK), lambda b, s: (b, 0, idx(s)))]

    out_f = pl.BlockSpec((nb, MIX_BLOCK, 2 * MIX_W), lambda b, s: (b, jnp.maximum(s - n_cb, 0), 0))
    out_b = pl.BlockSpec((nb, MIX_BLOCK, 2 * MIX_W),
                         lambda b, s: (b, jnp.minimum(n_blk - 1 - s, n_blk - 1 - n_cb), 0))
    args_dir = [p, p, p, p, qk, gates, gates_t]
    return pl.pallas_call(
        functools.partial(_mix_kernel, n_cb),
        grid=(bsz // nb, n_blk),
        in_specs=[pl.BlockSpec((2, 2, MIX_W), lambda b, s: (0, 0, 0))]
        + dir_specs(f_idx, COL_HF0) + dir_specs(b_idx, COL_HF1),
        out_specs=[out_f, out_b],
        out_shape=[jax.ShapeDtypeStruct((bsz, n_lat, 2 * MIX_W), BF16)] * 2,
        scratch_shapes=[pltpu.VMEM((nb, 2, N_HEADS, D_HEAD, D_HEAD), F32),
                        pltpu.VMEM((nb, 2, N_HEADS, D_HEAD, D_HEAD), F32),
                        pltpu.VMEM((nb, 2, N_HEADS, 1, D_HEAD), F32),
                        pltpu.VMEM((nb, 2, 1, LANES), F32)],
        compiler_params=_cparams(("arbitrary", "arbitrary")),
        name="mix",
    )(lb_logits, *args_dir, *args_dir)


def _out_kernel(of_ref, ob_ref, hg_ref, mo_ref, x_ref, mod_ref, w_ref, hgn_ref, mln_ref, lng_ref, lnb_ref,
                x1_ref, h2t_ref, h2t_bf_ref):
    o = of_ref[0].astype(F32) + ob_ref[0].astype(F32)
    hg = hg_ref[0]
    mo = mo_ref[0]
    parts = []
    for h in range(N_HEADS):
        sl = slice(h * D_HEAD, (h + 1) * D_HEAD)
        oh = o[:, sl]
        y = oh * lax.rsqrt(jnp.mean(oh * oh, axis=-1, keepdims=True) + LN_EPS) * hgn_ref[:, sl]
        parts.append(y * _silu(hg[:, sl]))
    for h in range(N_HEADS):
        sl = slice(h * D_HEAD, (h + 1) * D_HEAD)
        oh = o[:, MIX_W + h * D_HEAD:MIX_W + (h + 1) * D_HEAD]
        parts.append(_ln(oh) * mln_ref[:, sl] * _sigmoid(mo[:, sl]))
    mix = jnp.concatenate(parts, axis=-1).astype(BF16)
    y = _mm(mix, w_ref[...])
    g1 = mod_ref[0, :, 0:D_MODEL]
    sh2 = mod_ref[0, :, D_MODEL:2 * D_MODEL]
    sc2 = mod_ref[0, :, 2 * D_MODEL:3 * D_MODEL]
    x1 = _ln(DEEPNORM_ALPHA * x_ref[0] + g1 * y) * lng_ref[...] + lnb_ref[...]
    x1_ref[0] = x1
    h2t = (_ln(x1) * (1.0 + sc2) + sh2).T
    h2t_ref[...] = h2t
    h2t_bf_ref[...] = h2t.astype(BF16)


def _readout_project(o_f, o_b, p, x, mod2, w_out, hg_norm_g, ml_norm_g, ln1_g, ln1_b, n_ctx):
    bsz, n_lat, _ = x.shape
    n_tiles = n_lat // ROW_TILE
    off = n_ctx // ROW_TILE
    row = lambda a: a.reshape(1, -1)
    const2 = lambda b, t: (0, 0)
    return pl.pallas_call(
        _out_kernel,
        grid=(bsz, n_tiles),
        in_specs=[pl.BlockSpec((1, ROW_TILE, 2 * MIX_W), lambda b, t: (b, t, 0)),
                  pl.BlockSpec((1, ROW_TILE, 2 * MIX_W), lambda b, t: (b, t, 0)),
                  pl.BlockSpec((1, ROW_TILE, MIX_W), lambda b, t: (b, t + off, COL_HG)),
                  pl.BlockSpec((1, ROW_TILE, MIX_W), lambda b, t: (b, t + off, COL_MO)),
                  pl.BlockSpec((1, ROW_TILE, D_MODEL), lambda b, t: (b, t, 0)),
                  pl.BlockSpec((1, 1, 3 * D_MODEL), lambda b, t: (b, 0, 0)),
                  pl.BlockSpec((2 * MIX_W, D_MODEL), const2),
                  pl.BlockSpec((1, MIX_W), const2),
                  pl.BlockSpec((1, MIX_W), const2),
                  pl.BlockSpec((1, D_MODEL), const2),
                  pl.BlockSpec((1, D_MODEL), const2)],
        out_specs=[pl.BlockSpec((1, ROW_TILE, D_MODEL), lambda b, t: (b, t, 0)),
                   pl.BlockSpec((D_MODEL, ROW_TILE), lambda b, t: (0, b * n_tiles + t)),
                   pl.BlockSpec((D_MODEL, ROW_TILE), lambda b, t: (0, b * n_tiles + t))],
        out_shape=[jax.ShapeDtypeStruct((bsz, n_lat, D_MODEL), F32),
                   jax.ShapeDtypeStruct((D_MODEL, bsz * n_lat), F32),
                   jax.ShapeDtypeStruct((D_MODEL, bsz * n_lat), BF16)],
        compiler_params=_cparams(("arbitrary", "arbitrary")),
        name="readout",
    )(o_f, o_b, p, p, x, mod2, w_out.astype(BF16), row(hg_norm_g), row(ml_norm_g), row(ln1_g), row(ln1_b))


def _extract_exact(s):
    ridx = lax.broadcasted_iota(jnp.int32, s.shape, 0).astype(F32)
    rank = jnp.full(s.shape, float(PEER_TOPK), F32)
    vals = []
    for it in range(PEER_TOPK):
        m = jnp.max(s, axis=0, keepdims=True)
        first = jnp.min(jnp.where(s == m, ridx, float(s.shape[0])), axis=0, keepdims=True)
        hit = ridx == first
        rank = jnp.where(hit, float(it), rank)
        s = jnp.where(hit, -jnp.inf, s)
        vals.append(m)
    return rank, vals


def _extract_fast(s):
    rank = jnp.full(s.shape, float(PEER_TOPK), F32)
    vals = []
    for it in range(PEER_TOPK):
        m = jnp.max(s, axis=0, keepdims=True)
        hit = s == m
        rank = jnp.where(hit, float(it), rank)
        s = jnp.where(hit, -jnp.inf, s)
        vals.append(m)
    return rank, vals


def _count_true(mask):
    return jnp.sum(jnp.where(mask, 1.0, 0.0), axis=0, keepdims=True)


def _cand_grid(v1, v2):
    v1a = jnp.concatenate(v1, axis=0)
    v2a = jnp.concatenate(v2, axis=0)
    row = lax.broadcasted_iota(jnp.int32, (8, v1a.shape[1]), 0)
    blocks = [v1a[0:1] + v2a]
    for a in range(1, 8):
        nb = PEER_TOPK // (a + 1)
        blk = v1a[a:a + 1] + v2a[0:8]
        blocks.append(blk if nb >= 8 else jnp.where(row < nb, blk, -jnp.inf))
    blocks.append(v1a[8:16] + v2a[0:1])
    return jnp.concatenate(blocks, axis=0)


def _cand_counts(chosen):
    cnt = [_count_true(chosen[0:PEER_TOPK])]
    for a in range(1, 8):
        cnt.append(_count_true(chosen[PEER_TOPK + 8 * (a - 1):PEER_TOPK + 8 * a]))
    base = PEER_TOPK + 8 * 7
    for r in range(8):
        cnt.append(jnp.where(chosen[base + r:base + r + 1], 1.0, 0.0))
    return cnt


def _sel_emit(hd, s1, s2, is_rank1, rank2, v1, v2, cand, chosen, r2_ref, ci_ref, a_ref, b_ref):
    top = v1[0] + v2[0]
    z = jnp.sum(jnp.where(chosen, jnp.exp(cand - top), 0.0), axis=0, keepdims=True)
    ci = jnp.zeros_like(s1)
    for a, cnt_a in enumerate(_cand_counts(chosen)):
        ci = jnp.where(is_rank1(a), cnt_a, ci)
    r2_ref[hd] = rank2.astype(BF16)
    ci_ref[hd] = ci
    a_ref[hd] = jnp.exp(s1 - v1[0])
    b_ref[hd] = (jnp.exp(s2 - v2[0]) / z).astype(BF16)


def _sel_kernel(h_ref, wq_hi_ref, wq_lo_ref, k_hi_ref, k_lo_ref, r2_ref, ci_ref, a_ref, b_ref):
    h2t = h_ref[...]
    h_hi, h_lo = _split(h2t, 2)
    qt = (_mm(wq_hi_ref[...], h_hi) + (_mm(wq_lo_ref[...], h_hi) + _mm(wq_hi_ref[...], h_lo)))
    half = PEER_DQ // 2
    for hd in range(PEER_HEADS):
        s = []
        for p_ in range(2):
            qh = qt[hd * PEER_DQ + p_ * half: hd * PEER_DQ + (p_ + 1) * half, :]
            q_hi, q_lo = _split(qh, 2)
            k_hi = k_hi_ref[hd, p_]
            k_lo = k_lo_ref[hd, p_]
            s.append(_mm(k_hi, q_hi) + (_mm(k_lo, q_hi) + _mm(k_hi, q_lo)))
        s1, s2 = s

        _, v1 = _extract_fast(s1)
        rank2, v2 = _extract_fast(s2)
        cand = _cand_grid(v1, v2)
        _, cv = _extract_fast(cand)
        chosen = cand >= cv[PEER_TOPK - 1]
        k = float(PEER_TOPK)
        tied = ((_count_true(s1 >= v1[PEER_TOPK - 1]) != k) | (_count_true(rank2 < k) != k)
                | (_count_true(chosen) != k))
        any_tied = jnp.max(jnp.where(tied, 1.0, 0.0)) > 0.0
        _sel_emit(hd, s1, s2, lambda a: s1 == v1[a], rank2, v1, v2, cand, chosen, r2_ref, ci_ref, a_ref, b_ref)

        @pl.when(any_tied)
        def _():
            rank1, v1 = _extract_exact(s1)
            rank2, v2 = _extract_exact(s2)
            cand = _cand_grid(v1, v2)
            crank, _ = _extract_exact(cand)
            _sel_emit(hd, s1, s2, lambda a: rank1 == float(a), rank2, v1, v2, cand, crank < k,
                      r2_ref, ci_ref, a_ref, b_ref)


def _peer_select(h2t, peer_wq, peer_keys):
    n_tok = h2t.shape[1]
    wqt = peer_wq.T
    wq_hi = wqt.astype(BF16)
    wq_lo = (wqt - wq_hi.astype(F32)).astype(BF16)
    k_hi = peer_keys.astype(BF16)
    k_lo = (peer_keys - k_hi.astype(F32)).astype(BF16)
    dq_all = PEER_HEADS * PEER_DQ
    out_spec = pl.BlockSpec((PEER_HEADS, PEER_NKEYS, SEL_TILE), lambda t: (0, 0, t))
    shape = (PEER_HEADS, PEER_NKEYS, n_tok)
    return pl.pallas_call(
        _sel_kernel,
        grid=(n_tok // SEL_TILE,),
        in_specs=[pl.BlockSpec((D_MODEL, SEL_TILE), lambda t: (0, t)),
                  pl.BlockSpec((dq_all, D_MODEL), lambda t: (0, 0)),
                  pl.BlockSpec((dq_all, D_MODEL), lambda t: (0, 0)),
                  pl.BlockSpec((PEER_HEADS, 2, PEER_NKEYS, PEER_DQ // 2), lambda t: (0, 0, 0, 0)),
                  pl.BlockSpec((PEER_HEADS, 2, PEER_NKEYS, PEER_DQ // 2), lambda t: (0, 0, 0, 0))],
        out_specs=[out_spec] * 4,
        out_shape=[jax.ShapeDtypeStruct(shape, BF16), jax.ShapeDtypeStruct(shape, F32),
                   jax.ShapeDtypeStruct(shape, F32), jax.ShapeDtypeStruct(shape, BF16)],
        compiler_params=_cparams(("arbitrary",)),
        name="peersel",
    )(h2t, wq_hi, wq_lo, k_hi, k_lo)


def _rows_bf16(row):
    tile = jnp.broadcast_to(row, (16, row.shape[1])).astype(BF16)
    return jnp.concatenate([tile] * (PEER_NKEYS // 16), axis=0)


def _peer_kernel(n_e, h_ref, r2_ref, b_ref, ci_ref, a_ref, u_ref, vt_ref, x1_ref, g2_ref, lng_ref, lnb_ref,
                 o_ref, acc_ref, pre_ref, wa_ref):
    e = pl.program_id(1)

    @pl.when(e == 0)
    def _():
        acc_ref[...] = jnp.zeros_like(acc_ref)

    pre_ref[...] = _mm(u_ref[...], h_ref[...])
    zero = jnp.zeros((), BF16)
    for ii in range(PEER_ET // PEER_NKEYS):
        w = None
        for hd in range(PEER_HEADS):
            cnt = _rows_bf16(ci_ref[hd, ii:ii + 1, :])
            fac = _rows_bf16(a_ref[hd, ii:ii + 1, :])
            term = jnp.where(r2_ref[hd] < cnt, b_ref[hd] * fac, zero)
            w = term if w is None else w + term
        rows = pl.ds(ii * PEER_NKEYS, PEER_NKEYS)
        wa_ref[rows, :] = w * _gelu(pre_ref[rows, :].astype(BF16))
    acc_ref[...] += _mm(vt_ref[...], wa_ref[...])

    @pl.when(e == n_e - 1)
    def _():
        y = acc_ref[...].T
        o_ref[0] = _ln(DEEPNORM_ALPHA * x1_ref[0] + g2_ref[0] * y) * lng_ref[...] + lnb_ref[...]


def _peer_dense(h2t_bf, r2, ci, a_fac, b_fac, u_bf, vt_bf, x1, g2, ln2_g, ln2_b):
    bsz, n_lat, _ = x1.shape
    n_tok = bsz * n_lat
    n_t = n_tok // PEER_TT
    per_b = n_lat // PEER_TT
    n_e = u_bf.shape[0] // PEER_ET
    per = PEER_ET // PEER_NKEYS
    assert per % 8 == 0, "whole sublane tiles of half-1 keys per expert step"
    tok_spec = pl.BlockSpec((PEER_HEADS, PEER_NKEYS, PEER_TT), lambda t, e: (0, 0, t))
    key_spec = pl.BlockSpec((PEER_HEADS, per, PEER_TT), lambda t, e: (0, e, t))
    const2 = lambda t, e: (0, 0)
    return pl.pallas_call(
        functools.partial(_peer_kernel, n_e),
        grid=(n_t, n_e),
        in_specs=[pl.BlockSpec((D_MODEL, PEER_TT), lambda t, e: (0, t)),
                  tok_spec, tok_spec, key_spec, key_spec,
                  pl.BlockSpec((PEER_ET, D_MODEL), lambda t, e: (e, 0)),
                  pl.BlockSpec((D_MODEL, PEER_ET), lambda t, e: (0, e)),
                  pl.BlockSpec((1, PEER_TT, D_MODEL), lambda t, e: (t // per_b, t % per_b, 0)),
                  pl.BlockSpec((1, 1, D_MODEL), lambda t, e: (t // per_b, 0, 0)),
                  pl.BlockSpec((1, D_MODEL), const2),
                  pl.BlockSpec((1, D_MODEL), const2)],
        out_specs=pl.BlockSpec((1, PEER_TT, D_MODEL), lambda t, e: (t // per_b, t % per_b, 0)),
        out_shape=jax.ShapeDtypeStruct((bsz, n_lat, D_MODEL), F32),
        scratch_shapes=[pltpu.VMEM((D_MODEL, PEER_TT), F32),
                        pltpu.VMEM((PEER_ET, PEER_TT), F32),
                        pltpu.VMEM((PEER_ET, PEER_TT), BF16)],
        compiler_params=_cparams(("arbitrary", "arbitrary")),
        name="peer",
    )(h2t_bf, r2, b_fac, ci, a_fac, u_bf, vt_bf, x1, g2, ln2_g.reshape(1, -1), ln2_b.reshape(1, -1))


def kernel(x, c, ctx, c_ctx, w_mod, b_mod, w_in, hg_lb_logits, hg_norm_g, ml_conv_w, ml_conv_b, ml_gate_b,
           ml_norm_g, w_out, ln1_g, ln1_b, peer_wq, peer_keys, peer_u, peer_v, ln2_g, ln2_b):
    bsz, n_lat, _ = x.shape
    n_ctx = ctx.shape[1]
    assert n_ctx % ROW_TILE == 0 and n_lat % ROW_TILE == 0 and ROW_TILE % CHUNK == 0
    assert n_ctx % MIX_BLOCK == 0 and n_lat % MIX_BLOCK == 0 and MIX_BLOCK % CHUNK == 0
    assert w_mod.shape[0] == 1, "single-layer kernel"
    lyr = 0

    n_rows = -(-(bsz + 1) // 8) * 8
    c_all = jnp.zeros((n_rows, D_MODEL), F32).at[:bsz].set(c).at[bsz].set(c_ctx)
    mod = _modulation(c_all, w_mod[lyr], b_mod[lyr])
    mod_l, mod_c = mod[:bsz], mod[bsz]
    n_tiles_ctx = n_ctx // ROW_TILE
    n_tiles = (n_ctx + n_lat) // ROW_TILE
    mod1_l = mod_l[:, None, :2 * D_MODEL]
    mod1_c = jnp.broadcast_to(mod_c[None, None, :2 * D_MODEL], (bsz, 1, 2 * D_MODEL))
    mod1 = jnp.concatenate([jnp.repeat(mod1_c, n_tiles_ctx, axis=1),
                            jnp.repeat(mod1_l, n_tiles - n_tiles_ctx, axis=1)], axis=1)[:, :, None, :]
    mod2 = mod_l[:, None, 2 * D_MODEL:5 * D_MODEL]
    g2 = mod_l[:, None, 5 * D_MODEL:6 * D_MODEL]

    w_main = w_in[lyr][:, :D_MAIN].astype(BF16)
    w_gates = w_in[lyr][:, D_MAIN:]
    p, gates, gates_t = _input_projection(ctx, x, mod1, w_main, w_gates, ml_gate_b[lyr])

    qk = _qk_conv(p, ml_conv_w[lyr], ml_conv_b[lyr], n_ctx)
    o_f, o_b = _token_mixers(p, qk, gates, gates_t, hg_lb_logits[:, lyr:lyr + 2], n_ctx)
    x1, h2t, h2t_bf = _readout_project(o_f, o_b, p, x, mod2, w_out[lyr], hg_norm_g[lyr], ml_norm_g[lyr],
                                       ln1_g[lyr], ln1_b[lyr], n_ctx)
    r2, ci, a_fac, b_fac = _peer_select(h2t, peer_wq[lyr], peer_keys[lyr])
    u_bf = peer_u[lyr].astype(BF16)
    vt_bf = peer_v[lyr].astype(BF16).T
    return _peer_dense(h2t_bf, r2, ci, a_fac, b_fac, u_bf, vt_bf, x1, g2, ln2_g[lyr], ln2_b[lyr])
```

```python
import functools

import jax
import jax.numpy as jnp
from jax import lax
from jax.experimental import pallas as pl
from jax.experimental.pallas import tpu as pltpu

F32 = jnp.float32
BF16 = jnp.bfloat16

D_MODEL = 1024
CHUNK = 64
GRID_W = 64
N_HEADS = 4
D_HEAD = 128
MIX_W = N_HEADS * D_HEAD
N_GATES = 4 * N_HEADS
D_MAIN = 9 * MIX_W
PEER_HEADS = 8
PEER_NKEYS = 128
PEER_TOPK = 16
PEER_DQ = 256
LN_EPS = 1e-6
DEEPNORM_ALPHA = 2.0 ** 0.25
LANES = 128
VMEM_LIMIT = 56 * 1024 * 1024

COL_HQ, COL_HI, COL_HG, COL_HF0, COL_HF1, COL_MQ, COL_MK, COL_MV, COL_MO = range(9)

ROW_TILE = 256
SEL_TILE = 256
PEER_TT = 512
PEER_ET = 2048

NT_DIMS = (((1,), (1,)), ((), ()))
TN_DIMS = (((0,), (0,)), ((), ()))


def _cparams(sem):
    return pltpu.CompilerParams(dimension_semantics=sem, vmem_limit_bytes=VMEM_LIMIT)


def _split(a, n):
    parts = []
    r = a
    for k in range(n):
        p = r.astype(BF16)
        parts.append(p)
        if k + 1 < n:
            r = r - p.astype(F32)
    return parts


def _mm(a, b, dims=None):
    if dims is None:
        return jnp.dot(a, b, preferred_element_type=F32)
    return lax.dot_general(a, b, dims, preferred_element_type=F32)


def _mm_x3(a, b, dims=None):
    a_hi, a_lo = _split(a, 2)
    b_hi, b_lo = _split(b, 2)
    return _mm(a_hi, b_hi, dims) + (_mm(a_hi, b_lo, dims) + _mm(a_lo, b_hi, dims))


def _mm_ones_left(m_bf16, x):
    x0, x1 = _split(x, 2)
    return _mm(m_bf16, x0) + _mm(m_bf16, x1)


def _mm_ones_right(x, m_bf16):
    x0, x1 = _split(x, 2)
    return _mm(x0, m_bf16) + _mm(x1, m_bf16)


def _sigmoid(x):
    return 1.0 / (1.0 + jnp.exp(-x))


def _silu(x):
    return x * _sigmoid(x)


def _log_sigmoid(x):
    return jnp.minimum(x, 0.0) - jnp.log(1.0 + jnp.exp(-jnp.abs(x)))


def _ln(x):
    mu = jnp.mean(x, axis=-1, keepdims=True)
    xc = x - mu
    var = jnp.mean(xc * xc, axis=-1, keepdims=True)
    return xc * lax.rsqrt(var + LN_EPS)


def _gelu(x):
    return 0.5 * x * (1.0 + lax.erf(x * (2.0 ** -0.5)))


def _mod_kernel(c_ref, w_ref, b_ref, o_ref):
    s = _silu(c_ref[...])
    o_ref[...] = _mm_x3(s, w_ref[...]) + b_ref[...]


def _modulation(c_all, w_mod, b_mod):
    n = c_all.shape[0]
    d_out = w_mod.shape[1]
    blk = 1024
    return pl.pallas_call(
        _mod_kernel,
        grid=(d_out // blk,),
        in_specs=[pl.BlockSpec((n, D_MODEL), lambda j: (0, 0)),
                  pl.BlockSpec((D_MODEL, blk), lambda j: (0, j)),
                  pl.BlockSpec((1, blk), lambda j: (0, j))],
        out_specs=pl.BlockSpec((n, blk), lambda j: (0, j)),
        out_shape=jax.ShapeDtypeStruct((n, d_out), F32),
        compiler_params=_cparams(("arbitrary",)),
        name="mod",
    )(c_all, w_mod, b_mod.reshape(1, d_out))


def _inproj_kernel(n_ctx_tiles, ctx_ref, x_ref, mod_ref, w_ref, gb_ref, p_ref, g_ref, gt_ref):
    x = jnp.where(pl.program_id(1) < n_ctx_tiles, ctx_ref[0], x_ref[0])
    shift = mod_ref[0, 0, :, :D_MODEL]
    scale = mod_ref[0, 0, :, D_MODEL:]
    h = _ln(x) * (1.0 + scale) + shift
    h_hi, h_lo = _split(h, 2)
    res = _mm(h_hi, w_ref[...])
    p_ref[0] = res[:, :D_MAIN]
    g = (res[:, D_MAIN:D_MAIN + LANES]
         + (res[:, D_MAIN + LANES:] + _mm(h_lo, w_ref[:, D_MAIN:D_MAIN + LANES]))) + gb_ref[...]
    g_ref[0] = g
    gt_ref[0] = g.T[:N_GATES, :]


def _input_projection(ctx, x, mod1, w_main, wg, gate_b):
    bsz, n_ctx, _ = ctx.shape
    n_ctx_tiles = n_ctx // ROW_TILE
    t_all = n_ctx + x.shape[1]
    n_tiles = t_all // ROW_TILE
    wg_pad = jnp.zeros((D_MODEL, LANES), F32).at[:, :N_GATES].set(wg)
    wg_hi = wg_pad.astype(BF16)
    wg_lo = (wg_pad - wg_hi.astype(F32)).astype(BF16)
    w_cat = jnp.concatenate([w_main, wg_hi, wg_lo], axis=1)
    gb = jnp.zeros((1, LANES), F32).at[0, :N_GATES].set(gate_b)
    const2 = lambda b, t: (0, 0)
    return pl.pallas_call(
        functools.partial(_inproj_kernel, n_ctx_tiles),
        grid=(bsz, n_tiles),
        in_specs=[pl.BlockSpec((1, ROW_TILE, D_MODEL), lambda b, t: (b, jnp.minimum(t, n_ctx_tiles - 1), 0)),
                  pl.BlockSpec((1, ROW_TILE, D_MODEL), lambda b, t: (b, jnp.maximum(t - n_ctx_tiles, 0), 0)),
                  pl.BlockSpec((1, 1, 1, 2 * D_MODEL), lambda b, t: (b, t, 0, 0)),
                  pl.BlockSpec((D_MODEL, D_MAIN + 2 * LANES), const2),
                  pl.BlockSpec((1, LANES), const2)],
        out_specs=[pl.BlockSpec((1, ROW_TILE, D_MAIN), lambda b, t: (b, t, 0)),
                   pl.BlockSpec((1, ROW_TILE, LANES), lambda b, t: (b, t, 0)),
                   pl.BlockSpec((1, N_GATES, ROW_TILE), lambda b, t: (b, 0, t))],
        out_shape=[jax.ShapeDtypeStruct((bsz, t_all, D_MAIN), F32),
                   jax.ShapeDtypeStruct((bsz, t_all, LANES), F32),
                   jax.ShapeDtypeStruct((bsz, N_GATES, t_all), F32)],
        compiler_params=_cparams(("arbitrary", "arbitrary")),
        name="inproj",
    )(ctx, x, mod1, w_cat, gb)


CONV_PAD = 72
CONV_CB = 256


def _conv_kernel(n_ctx, n_lat, p_ref, w_ref, b_ref, o_ref, pad_c, pad_l):
    cb = pl.program_id(1)
    w = w_ref[...]
    bias = b_ref[...]
    scale = jnp.where(cb >= MIX_W // CONV_CB, D_HEAD ** -0.5, 1.0).astype(F32)

    def finish(y):
        return _silu(y + bias) * scale

    pad_c[...] = jnp.zeros_like(pad_c)
    pad_c[pl.ds(CONV_PAD, n_ctx), :] = p_ref[0, pl.ds(0, n_ctx), :]
    y = jnp.zeros((n_ctx, CONV_CB), F32)
    for dc in range(3):
        y = y + pad_c[pl.ds(CONV_PAD + dc - 1, n_ctx), :] * w[3 + dc:4 + dc, :]
    o_ref[0, pl.ds(0, n_ctx), :] = finish(y)

    x = p_ref[0, pl.ds(n_ctx, n_lat), :]
    col = lax.broadcasted_iota(jnp.int32, (n_lat, CONV_CB), 0) & (GRID_W - 1)
    pad_l[...] = jnp.zeros_like(pad_l)
    pad_l[0, pl.ds(CONV_PAD + 1, n_lat), :] = jnp.where(col == GRID_W - 1, 0.0, x)
    pad_l[1, pl.ds(CONV_PAD, n_lat), :] = x
    pad_l[2, pl.ds(CONV_PAD - 1, n_lat), :] = jnp.where(col == 0, 0.0, x)
    y = jnp.zeros((n_lat, CONV_CB), F32)
    for dr in range(3):
        for dc in range(3):
            rows = pl.ds(CONV_PAD + (dr - 1) * GRID_W, n_lat)
            y = y + pad_l[dc, rows, :] * w[3 * dr + dc:3 * dr + dc + 1, :]
    o_ref[0, pl.ds(n_ctx, n_lat), :] = finish(y)


def _qk_conv(p, conv_w, conv_b, n_ctx):
    bsz, t_all, _ = p.shape
    n_lat = t_all - n_ctx
    n_cb = 2 * MIX_W // CONV_CB
    first = COL_MQ * MIX_W // CONV_CB
    w9 = conv_w.reshape(9, 2 * MIX_W)
    return pl.pallas_call(
        functools.partial(_conv_kernel, n_ctx, n_lat),
        grid=(bsz, n_cb),
        in_specs=[pl.BlockSpec((1, t_all, CONV_CB), lambda b, c: (b, 0, first + c)),
                  pl.BlockSpec((9, CONV_CB), lambda b, c: (0, c)),
                  pl.BlockSpec((1, CONV_CB), lambda b, c: (0, c))],
        out_specs=pl.BlockSpec((1, t_all, CONV_CB), lambda b, c: (b, 0, c)),
        out_shape=jax.ShapeDtypeStruct((bsz, t_all, 2 * MIX_W), F32),
        scratch_shapes=[pltpu.VMEM((n_ctx + 2 * CONV_PAD, CONV_CB), F32),
                        pltpu.VMEM((3, n_lat + 2 * CONV_PAD, CONV_CB), F32)],
        compiler_params=_cparams(("arbitrary", "arbitrary")),
        name="qkconv",
    )(p, w9, conv_b.reshape(1, 2 * MIX_W))


MIX_BLOCK = 256
MIX_BATCH = 2


def _block_mask(t, rev):
    r = lax.broadcasted_iota(jnp.int32, (t, t), 0)
    c = lax.broadcasted_iota(jnp.int32, (t, t), 1)
    shift = CHUNK.bit_length() - 1
    same = (r >> shift) == (c >> shift)
    return jnp.logical_and(same, (c >= r) if rev else (c <= r))


def _chunk_rows(x, idx):
    return jnp.concatenate(
        [jnp.broadcast_to(x[j * CHUNK + idx:j * CHUNK + idx + 1], (CHUNK, x.shape[1]))
         for j in range(x.shape[0] // CHUNK)], axis=0)


def _scan_order(n, rev):
    return range(n - 1, -1, -1) if rev else range(n)


def _chunk_expand(x):
    n = x.shape[0] // CHUNK
    chunk = lax.broadcasted_iota(jnp.int32, x.shape, 0) >> (CHUNK.bit_length() - 1)
    return jnp.concatenate([jnp.where(chunk == j, x, jnp.zeros((), x.dtype)) for j in range(n)], axis=1)


def _diag_blocks(y, w):
    n = y.shape[0] // CHUNK
    return jnp.concatenate([y[j * CHUNK:(j + 1) * CHUNK, j * w:(j + 1) * w] for j in range(n)], axis=0)


def _hgrn_dir(d, rev, want_out, q_ref, v_ref, z_ref, lb, st_ref, o_ref):
    vb = v_ref[0].astype(BF16)
    sig = _sigmoid(z_ref[0])
    log_f = jnp.log(lb + (1.0 - lb) * sig)
    kk = (1.0 - lb) * (1.0 - sig)
    t = vb.shape[0]
    n_sub = t // CHUNK
    mask = _block_mask(t, rev)
    tri = jnp.where(mask, 1.0, 0.0).astype(BF16)
    b = _mm_ones_left(tri, log_f)
    last = 0 if rev else CHUNK - 1
    mid = CHUNK // 2 - 1 if rev else CHUNK // 2
    kdec = (kk * jnp.exp(_chunk_rows(b, last) - b)).astype(BF16)
    if want_out:
        q = q_ref[0]
        b_mid = _chunk_rows(b, mid)
        qd = (q * jnp.exp(b - b_mid)).astype(BF16)
        kd = (kk * jnp.exp(b_mid - b)).astype(BF16)
        qe = (q * jnp.exp(b)).astype(BF16)
    for h in range(N_HEADS):
        sl = slice(h * D_HEAD, (h + 1) * D_HEAD)
        upd = _mm(vb[:, sl], _chunk_expand(kdec[:, sl]), TN_DIMS)
        st = st_ref[d, h]
        before = [None] * n_sub
        for j in _scan_order(n_sub, rev):
            before[j] = st
            e_last = jnp.exp(b[j * CHUNK + last:j * CHUNK + last + 1, sl])
            st = st * e_last + upd[:, j * D_HEAD:(j + 1) * D_HEAD]
        st_ref[d, h] = st
        if want_out:
            sc = _mm(qd[:, sl], kd[:, sl], NT_DIMS)
            o = _mm(jnp.where(mask, sc, 0.0).astype(BF16), vb[:, sl])
            stacked = jnp.concatenate([s_.astype(BF16) for s_ in before], axis=0)
            inter = _diag_blocks(_mm(qe[:, sl], stacked, NT_DIMS), D_HEAD)
            o_ref[0, :, sl] = (o + inter).astype(o_ref.dtype)


def _mlstm_dir(d, rev, want_out, qk_ref, v_ref, gc_ref, gr_ref, c_ref, n_ref, m_ref, o_ref):
    qk = qk_ref[0]
    vb = v_ref[0].astype(BF16)
    gc = gc_ref[0]
    gr = gr_ref[0]
    t = gc.shape[0]
    n_sub = t // CHUNK
    mask = _block_mask(t, rev)
    tri = jnp.where(mask, 1.0, 0.0).astype(BF16)
    a_cols = _mm_ones_left(tri, _log_sigmoid(gc))
    if want_out:
        tri_t = jnp.where(_block_mask(t, not rev), 1.0, 0.0).astype(BF16)
        a_rows = _mm_ones_right(_log_sigmoid(gr), tri_t)
    last = 0 if rev else CHUNK - 1
    order = _scan_order(n_sub, rev)

    log_i = pltpu.roll(gc, 2 * N_HEADS, axis=1)
    g = _chunk_rows(a_cols, last) - a_cols + log_i
    m_loc = [jnp.max(g[j * CHUNK:(j + 1) * CHUNK], axis=0, keepdims=True) for j in range(n_sub)]
    w_in = jnp.exp(g - jnp.concatenate([jnp.broadcast_to(r, (CHUNK, LANES)) for r in m_loc], axis=0))
    m = m_ref[d]
    m_before, s_old, s_new = [None] * n_sub, [None] * n_sub, [None] * n_sub
    for j in order:
        a_end = a_cols[j * CHUNK + last:j * CHUNK + last + 1]
        m_before[j] = m
        m_new = jnp.maximum(a_end + m, m_loc[j])
        s_old[j] = jnp.exp(a_end + m - m_new)
        s_new[j] = jnp.exp(m_loc[j] - m_new)
        m = m_new
    m_ref[d] = m
    m_prev_all = jnp.concatenate([jnp.broadcast_to(r, (CHUNK, LANES)) for r in m_before], axis=0)

    for h in range(N_HEADS):
        ic = d * N_HEADS + h
        fc = 2 * N_HEADS + d * N_HEADS + h
        q_h = qk[:, h * D_HEAD:(h + 1) * D_HEAD]
        k_h = qk[:, MIX_W + h * D_HEAD:MIX_W + (h + 1) * D_HEAD]
        v_h = vb[:, h * D_HEAD:(h + 1) * D_HEAD]
        qb = q_h.astype(BF16)
        wk = w_in[:, fc:fc + 1] * k_h
        wkb = wk.astype(BF16)
        upd = _mm(wkb, _chunk_expand(v_h), TN_DIMS)
        c_st = c_ref[d, h]
        n_st = n_ref[d, h]
        c_before, qn = [None] * n_sub, [None] * n_sub
        for j in order:
            rows = slice(j * CHUNK, (j + 1) * CHUNK)
            c_before[j] = c_st
            if want_out:
                qn[j] = jnp.sum(q_h[rows] * n_st, axis=1, keepdims=True)
            so = s_old[j][:, fc:fc + 1]
            sn = s_new[j][:, fc:fc + 1]
            c_st = so * c_st + sn * upd[:, j * D_HEAD:(j + 1) * D_HEAD]
            n_st = so * n_st + sn * jnp.sum(wk[rows], axis=0, keepdims=True)
        c_ref[d, h] = c_st
        n_ref[d, h] = n_st
        if not want_out:
            continue
        side = jnp.concatenate([c_.astype(BF16) for c_ in c_before], axis=1)
        inter = _diag_blocks(_mm(qb, side), D_HEAD)

        r_row = gr[ic:ic + 1, :] - a_rows[fc:fc + 1, :]
        m_prev = m_prev_all[:, fc:fc + 1]
        r_max = jnp.max(jnp.where(mask, r_row, -jnp.inf), axis=1, keepdims=True)
        mm = jnp.maximum(r_max, m_prev)
        w_intra = _mm(qb, k_h.astype(BF16), NT_DIMS) * jnp.exp(jnp.where(mask, r_row - mm, -jnp.inf))
        w_inter = jnp.exp(m_prev - mm)
        num = _mm(w_intra.astype(BF16), v_h) + w_inter * inter
        den = jnp.sum(w_intra, axis=1, keepdims=True) + w_inter * jnp.concatenate(qn, axis=0)
        floor = jnp.exp(-(a_cols[:, fc:fc + 1] + mm))
        o_ref[0, :, MIX_W + h * D_HEAD:MIX_W + (h + 1) * D_HEAD] = (
            num / jnp.maximum(jnp.abs(den), floor)).astype(o_ref.dtype)


def _mix_kernel(n_cb, lbl_ref,
                hq_f, hi_f, hf_f, mv_f, qk_f, gc_f, gr_f,
                hq_b, hi_b, hf_b, mv_b, qk_b, gc_b, gr_b,
                of_ref, ob_ref, st_ref, c_ref, n_ref, m_ref):
    s = pl.program_id(1)

    @pl.when(s == 0)
    def _():
        st_ref[...] = jnp.zeros_like(st_ref)
        c_ref[...] = jnp.zeros_like(c_ref)
        n_ref[...] = jnp.zeros_like(n_ref)
        m_ref[...] = jnp.zeros_like(m_ref)

    def step(want_out):
        for d, rev, dir_refs in (
                (0, False, (hq_f, hi_f, hf_f, mv_f, qk_f, gc_f, gr_f, of_ref)),
                (1, True, (hq_b, hi_b, hf_b, mv_b, qk_b, gc_b, gr_b, ob_ref))):
            logits = lbl_ref[d]
            mx = jnp.max(logits, axis=0, keepdims=True)
            ex = jnp.exp(logits - mx)
            lb = ex[0:1, :] / jnp.sum(ex, axis=0, keepdims=True)
            for bb in range(MIX_BATCH):
                hq, hi, hf, mv, qk, gc, gr, o_ref = (r.at[pl.ds(bb, 1)] for r in dir_refs)
                _hgrn_dir(d, rev, want_out, hq, hi, hf, lb, st_ref.at[bb], o_ref)
                _mlstm_dir(d, rev, want_out, qk, mv, gc, gr, c_ref.at[bb], n_ref.at[bb], m_ref.at[bb], o_ref)

    @pl.when(s < n_cb)
    def _():
        step(False)

    @pl.when(s >= n_cb)
    def _():
        step(True)


def _token_mixers(p, qk, gates, gates_t, lb_logits, n_ctx):
    bsz, t_all, _ = p.shape
    n_lat = t_all - n_ctx
    n_cb = n_ctx // MIX_BLOCK
    n_blk = t_all // MIX_BLOCK

    def f_idx(s):
        return s

    def b_idx(s):
        return jnp.where(s < n_cb, n_cb - 1 - s, n_blk - 1 + n_cb - s)

    nb = MIX_BATCH
    assert bsz % nb == 0

    def pspec(col, idx):
        return pl.BlockSpec((nb, MIX_BLOCK, MIX_W), lambda b, s: (b, idx(s), col))

    def dir_specs(idx, col_f):
        return [pspec(COL_HQ, idx), pspec(COL_HI, idx), pspec(col_f, idx), pspec(COL_MV, idx),
                pl.BlockSpec((nb, MIX_BLOCK, 2 * MIX_W), lambda b, s: (b, idx(s), 0)),
                pl.BlockSpec((nb, MIX_BLOCK, LANES), lambda b, s: (b, idx(s), 0)),
                pl.BlockSpec((nb, N_GATES, MIX_BLOCK), lambda b, s: (b, 0, idx(s)))]

    out_f = pl.BlockSpec((nb, MIX_BLOCK, 2 * MIX_W), lambda b, s: (b, jnp.maximum(s - n_cb, 0), 0))
    out_b = pl.BlockSpec((nb, MIX_BLOCK, 2 * MIX_W),
                         lambda b, s: (b, jnp.minimum(n_blk - 1 - s, n_blk - 1 - n_cb), 0))
    args_dir = [p, p, p, p, qk, gates, gates_t]
    return pl.pallas_call(
        functools.partial(_mix_kernel, n_cb),
        grid=(bsz // nb, n_blk),
        in_specs=[pl.BlockSpec((2, 2, MIX_W), lambda b, s: (0, 0, 0))]
        + dir_specs(f_idx, COL_HF0) + dir_specs(b_idx, COL_HF1),
        out_specs=[out_f, out_b],
        out_shape=[jax.ShapeDtypeStruct((bsz, n_lat, 2 * MIX_W), BF16)] * 2,
        scratch_shapes=[pltpu.VMEM((nb, 2, N_HEADS, D_HEAD, D_HEAD), F32),
                        pltpu.VMEM((nb, 2, N_HEADS, D_HEAD, D_HEAD), F32),
                        pltpu.VMEM((nb, 2, N_HEADS, 1, D_HEAD), F32),
                        pltpu.VMEM((nb, 2, 1, LANES), F32)],
        compiler_params=_cparams(("arbitrary", "arbitrary")),
        name="mix",
    )(lb_logits, *args_dir, *args_dir)


def _out_kernel(of_ref, ob_ref, hg_ref, mo_ref, x_ref, mod_ref, w_ref, hgn_ref, mln_ref, lng_ref, lnb_ref,
                x1_ref, h2t_hi_ref, h2t_lo_ref):
    o = of_ref[0].astype(F32) + ob_ref[0].astype(F32)
    hg = hg_ref[0]
    mo = mo_ref[0]
    parts = []
    for h in range(N_HEADS):
        sl = slice(h * D_HEAD, (h + 1) * D_HEAD)
        oh = o[:, sl]
        y = oh * lax.rsqrt(jnp.mean(oh * oh, axis=-1, keepdims=True) + LN_EPS) * hgn_ref[:, sl]
        parts.append(y * _silu(hg[:, sl]))
    for h in range(N_HEADS):
        sl = slice(h * D_HEAD, (h + 1) * D_HEAD)
        oh = o[:, MIX_W + h * D_HEAD:MIX_W + (h + 1) * D_HEAD]
        parts.append(_ln(oh) * mln_ref[:, sl] * _sigmoid(mo[:, sl]))
    mix = jnp.concatenate(parts, axis=-1).astype(BF16)
    y = _mm(mix, w_ref[...])
    g1 = mod_ref[0, :, 0:D_MODEL]
    sh2 = mod_ref[0, :, D_MODEL:2 * D_MODEL]
    sc2 = mod_ref[0, :, 2 * D_MODEL:3 * D_MODEL]
    x1 = _ln(DEEPNORM_ALPHA * x_ref[0] + g1 * y) * lng_ref[...] + lnb_ref[...]
    x1_ref[0] = x1
    h_hi, h_lo = _split((_ln(x1) * (1.0 + sc2) + sh2).T, 2)
    h2t_hi_ref[...] = h_hi
    h2t_lo_ref[...] = h_lo


def _readout_project(o_f, o_b, p, x, mod2, w_out, hg_norm_g, ml_norm_g, ln1_g, ln1_b, n_ctx):
    bsz, n_lat, _ = x.shape
    n_tiles = n_lat // ROW_TILE
    off = n_ctx // ROW_TILE
    row = lambda a: a.reshape(1, -1)
    const2 = lambda b, t: (0, 0)
    return pl.pallas_call(
        _out_kernel,
        grid=(bsz, n_tiles),
        in_specs=[pl.BlockSpec((1, ROW_TILE, 2 * MIX_W), lambda b, t: (b, t, 0)),
                  pl.BlockSpec((1, ROW_TILE, 2 * MIX_W), lambda b, t: (b, t, 0)),
                  pl.BlockSpec((1, ROW_TILE, MIX_W), lambda b, t: (b, t + off, COL_HG)),
                  pl.BlockSpec((1, ROW_TILE, MIX_W), lambda b, t: (b, t + off, COL_MO)),
                  pl.BlockSpec((1, ROW_TILE, D_MODEL), lambda b, t: (b, t, 0)),
                  pl.BlockSpec((1, 1, 3 * D_MODEL), lambda b, t: (b, 0, 0)),
                  pl.BlockSpec((2 * MIX_W, D_MODEL), const2),
                  pl.BlockSpec((1, MIX_W), const2),
                  pl.BlockSpec((1, MIX_W), const2),
                  pl.BlockSpec((1, D_MODEL), const2),
                  pl.BlockSpec((1, D_MODEL), const2)],
        out_specs=[pl.BlockSpec((1, ROW_TILE, D_MODEL), lambda b, t: (b, t, 0)),
                   pl.BlockSpec((D_MODEL, ROW_TILE), lambda b, t: (0, b * n_tiles + t)),
                   pl.BlockSpec((D_MODEL, ROW_TILE), lambda b, t: (0, b * n_tiles + t))],
        out_shape=[jax.ShapeDtypeStruct((bsz, n_lat, D_MODEL), F32),
                   jax.ShapeDtypeStruct((D_MODEL, bsz * n_lat), BF16),
                   jax.ShapeDtypeStruct((D_MODEL, bsz * n_lat), BF16)],
        compiler_params=_cparams(("arbitrary", "arbitrary")),
        name="readout",
    )(o_f, o_b, p, p, x, mod2, w_out.astype(BF16), row(hg_norm_g), row(ml_norm_g), row(ln1_g), row(ln1_b))


def _extract_exact(s):
    ridx = lax.broadcasted_iota(jnp.int32, s.shape, 0).astype(F32)
    rank = jnp.full(s.shape, float(PEER_TOPK), F32)
    vals = []
    for it in range(PEER_TOPK):
        m = jnp.max(s, axis=0, keepdims=True)
        first = jnp.min(jnp.where(s == m, ridx, float(s.shape[0])), axis=0, keepdims=True)
        hit = ridx == first
        rank = jnp.where(hit, float(it), rank)
        s = jnp.where(hit, -jnp.inf, s)
        vals.append(m)
    return rank, vals


def _extract_fast(s):
    rank = jnp.full(s.shape, float(PEER_TOPK), F32)
    vals = []
    for it in range(PEER_TOPK):
        m = jnp.max(s, axis=0, keepdims=True)
        hit = s == m
        rank = jnp.where(hit, float(it), rank)
        s = jnp.where(hit, -jnp.inf, s)
        vals.append(m)
    return rank, vals


def _count_true(mask):
    return jnp.sum(jnp.where(mask, 1.0, 0.0), axis=0, keepdims=True)


def _cand_grid(v1, v2):
    v1a = jnp.concatenate(v1, axis=0)
    v2a = jnp.concatenate(v2, axis=0)
    row = lax.broadcasted_iota(jnp.int32, (8, v1a.shape[1]), 0)
    blocks = [v1a[0:1] + v2a]
    for a in range(1, 8):
        nb = PEER_TOPK // (a + 1)
        blk = v1a[a:a + 1] + v2a[0:8]
        blocks.append(blk if nb >= 8 else jnp.where(row < nb, blk, -jnp.inf))
    blocks.append(v1a[8:16] + v2a[0:1])
    return jnp.concatenate(blocks, axis=0)


def _cand_counts(chosen):
    cnt = [_count_true(chosen[0:PEER_TOPK])]
    for a in range(1, 8):
        cnt.append(_count_true(chosen[PEER_TOPK + 8 * (a - 1):PEER_TOPK + 8 * a]))
    base = PEER_TOPK + 8 * 7
    for r in range(8):
        cnt.append(jnp.where(chosen[base + r:base + r + 1], 1.0, 0.0))
    return cnt


def _sel_emit(hd, s1, s2, is_rank1, rank2, v1, v2, cand, chosen, r2_ref, ci_ref, a_ref, b_ref):
    top = v1[0] + v2[0]
    z = jnp.sum(jnp.where(chosen, jnp.exp(cand - top), 0.0), axis=0, keepdims=True)
    ci = jnp.zeros_like(s1)
    for a, cnt_a in enumerate(_cand_counts(chosen)):
        ci = jnp.where(is_rank1(a), cnt_a, ci)
    r2_ref[hd] = rank2.astype(BF16)
    ci_ref[hd] = ci
    a_ref[hd] = jnp.exp(s1 - v1[0])
    b_ref[hd] = (jnp.exp(s2 - v2[0]) / z).astype(BF16)


def _sel_kernel(h_hi_ref, h_lo_ref, wq_hi_ref, wq_lo_ref, k_hi_ref, k_lo_ref, r2_ref, ci_ref, a_ref, b_ref):
    h_hi = h_hi_ref[...]
    h_lo = h_lo_ref[...]
    qt = (_mm(wq_hi_ref[...], h_hi) + (_mm(wq_lo_ref[...], h_hi) + _mm(wq_hi_ref[...], h_lo)))
    half = PEER_DQ // 2
    for hd in range(PEER_HEADS):
        s = []
        for p_ in range(2):
            qh = qt[hd * PEER_DQ + p_ * half: hd * PEER_DQ + (p_ + 1) * half, :]
            q_hi, q_lo = _split(qh, 2)
            k_hi = k_hi_ref[hd, p_]
            k_lo = k_lo_ref[hd, p_]
            s.append(_mm(k_hi, q_hi) + (_mm(k_lo, q_hi) + _mm(k_hi, q_lo)))
        s1, s2 = s

        _, v1 = _extract_fast(s1)
        rank2, v2 = _extract_fast(s2)
        cand = _cand_grid(v1, v2)
        _, cv = _extract_fast(cand)
        chosen = cand >= cv[PEER_TOPK - 1]
        k = float(PEER_TOPK)
        tied = ((_count_true(s1 >= v1[PEER_TOPK - 1]) != k) | (_count_true(rank2 < k) != k)
                | (_count_true(chosen) != k))
        any_tied = jnp.max(jnp.where(tied, 1.0, 0.0)) > 0.0
        _sel_emit(hd, s1, s2, lambda a: s1 == v1[a], rank2, v1, v2, cand, chosen, r2_ref, ci_ref, a_ref, b_ref)

        @pl.when(any_tied)
        def _():
            rank1, v1 = _extract_exact(s1)
            rank2, v2 = _extract_exact(s2)
            cand = _cand_grid(v1, v2)
            crank, _ = _extract_exact(cand)
            _sel_emit(hd, s1, s2, lambda a: rank1 == float(a), rank2, v1, v2, cand, crank < k,
                      r2_ref, ci_ref, a_ref, b_ref)


def _peer_select(h2t_hi, h2t_lo, peer_wq, peer_keys):
    n_tok = h2t_hi.shape[1]
    wqt = peer_wq.T
    wq_hi = wqt.astype(BF16)
    wq_lo = (wqt - wq_hi.astype(F32)).astype(BF16)
    k_hi = peer_keys.astype(BF16)
    k_lo = (peer_keys - k_hi.astype(F32)).astype(BF16)
    dq_all = PEER_HEADS * PEER_DQ
    out_spec = pl.BlockSpec((PEER_HEADS, PEER_NKEYS, SEL_TILE), lambda t: (0, 0, t))
    shape = (PEER_HEADS, PEER_NKEYS, n_tok)
    return pl.pallas_call(
        _sel_kernel,
        grid=(n_tok // SEL_TILE,),
        in_specs=[pl.BlockSpec((D_MODEL, SEL_TILE), lambda t: (0, t)),
                  pl.BlockSpec((D_MODEL, SEL_TILE), lambda t: (0, t)),
                  pl.BlockSpec((dq_all, D_MODEL), lambda t: (0, 0)),
                  pl.BlockSpec((dq_all, D_MODEL), lambda t: (0, 0)),
                  pl.BlockSpec((PEER_HEADS, 2, PEER_NKEYS, PEER_DQ // 2), lambda t: (0, 0, 0, 0)),
                  pl.BlockSpec((PEER_HEADS, 2, PEER_NKEYS, PEER_DQ // 2), lambda t: (0, 0, 0, 0))],
        out_specs=[out_spec] * 4,
        out_shape=[jax.ShapeDtypeStruct(shape, BF16), jax.ShapeDtypeStruct(shape, F32),
                   jax.ShapeDtypeStruct(shape, F32), jax.ShapeDtypeStruct(shape, BF16)],
        compiler_params=_cparams(("arbitrary",)),
        name="peersel",
    )(h2t_hi, h2t_lo, wq_hi, wq_lo, k_hi, k_lo)


def _rows_bf16(row):
    tile = jnp.broadcast_to(row, (16, row.shape[1])).astype(BF16)
    return jnp.concatenate([tile] * (PEER_NKEYS // 16), axis=0)


def _peer_kernel(n_e, h_ref, r2_ref, b_ref, ci_ref, a_ref, u_ref, vt_ref, x1_ref, g2_ref, lng_ref, lnb_ref,
                 o_ref, acc_ref, pre_ref, wa_ref):
    e = pl.program_id(1)

    @pl.when(e == 0)
    def _():
        acc_ref[...] = jnp.zeros_like(acc_ref)

    pre_ref[...] = _mm(u_ref[...], h_ref[...])
    zero = jnp.zeros((), BF16)
    for ii in range(PEER_ET // PEER_NKEYS):
        w = None
        for hd in range(PEER_HEADS):
            cnt = _rows_bf16(ci_ref[hd, ii:ii + 1, :])
            fac = _rows_bf16(a_ref[hd, ii:ii + 1, :])
            term = jnp.where(r2_ref[hd] < cnt, b_ref[hd] * fac, zero)
            w = term if w is None else w + term
        rows = pl.ds(ii * PEER_NKEYS, PEER_NKEYS)
        wa_ref[rows, :] = w * _gelu(pre_ref[rows, :].astype(BF16))
    acc_ref[...] += _mm(vt_ref[...], wa_ref[...])

    @pl.when(e == n_e - 1)
    def _():
        y = acc_ref[...].T
        o_ref[0] = _ln(DEEPNORM_ALPHA * x1_ref[0] + g2_ref[0] * y) * lng_ref[...] + lnb_ref[...]


def _peer_dense(h2t_bf, r2, ci, a_fac, b_fac, u_bf, vt_bf, x1, g2, ln2_g, ln2_b):
    bsz, n_lat, _ = x1.shape
    n_tok = bsz * n_lat
    n_t = n_tok // PEER_TT
    per_b = n_lat // PEER_TT
    n_e = u_bf.shape[0] // PEER_ET
    per = PEER_ET // PEER_NKEYS
    assert per % 8 == 0, "whole sublane tiles of half-1 keys per expert step"
    tok_spec = pl.BlockSpec((PEER_HEADS, PEER_NKEYS, PEER_TT), lambda t, e: (0, 0, t))
    key_spec = pl.BlockSpec((PEER_HEADS, per, PEER_TT), lambda t, e: (0, e, t))
    const2 = lambda t, e: (0, 0)
    return pl.pallas_call(
        functools.partial(_peer_kernel, n_e),
        grid=(n_t, n_e),
        in_specs=[pl.BlockSpec((D_MODEL, PEER_TT), lambda t, e: (0, t)),
                  tok_spec, tok_spec, key_spec, key_spec,
                  pl.BlockSpec((PEER_ET, D_MODEL), lambda t, e: (e, 0)),
                  pl.BlockSpec((D_MODEL, PEER_ET), lambda t, e: (0, e)),
                  pl.BlockSpec((1, PEER_TT, D_MODEL), lambda t, e: (t // per_b, t % per_b, 0)),
                  pl.BlockSpec((1, 1, D_MODEL), lambda t, e: (t // per_b, 0, 0)),
                  pl.BlockSpec((1, D_MODEL), const2),
                  pl.BlockSpec((1, D_MODEL), const2)],
        out_specs=pl.BlockSpec((1, PEER_TT, D_MODEL), lambda t, e: (t // per_b, t % per_b, 0)),
        out_shape=jax.ShapeDtypeStruct((bsz, n_lat, D_MODEL), F32),
        scratch_shapes=[pltpu.VMEM((D_MODEL, PEER_TT), F32),
                        pltpu.VMEM((PEER_ET, PEER_TT), F32),
                        pltpu.VMEM((PEER_ET, PEER_TT), BF16)],
        compiler_params=_cparams(("arbitrary", "arbitrary")),
        name="peer",
    )(h2t_bf, r2, b_fac, ci, a_fac, u_bf, vt_bf, x1, g2, ln2_g.reshape(1, -1), ln2_b.reshape(1, -1))


def kernel(x, c, ctx, c_ctx, w_mod, b_mod, w_in, hg_lb_logits, hg_norm_g, ml_conv_w, ml_conv_b, ml_gate_b,
           ml_norm_g, w_out, ln1_g, ln1_b, peer_wq, peer_keys, peer_u, peer_v, ln2_g, ln2_b):
    bsz, n_lat, _ = x.shape
    n_ctx = ctx.shape[1]
    assert n_ctx % ROW_TILE == 0 and n_lat % ROW_TILE == 0 and ROW_TILE % CHUNK == 0
    assert n_ctx % MIX_BLOCK == 0 and n_lat % MIX_BLOCK == 0 and MIX_BLOCK % CHUNK == 0
    assert w_mod.shape[0] == 1, "single-layer kernel"
    lyr = 0

    n_rows = -(-(bsz + 1) // 8) * 8
    c_all = jnp.zeros((n_rows, D_MODEL), F32).at[:bsz].set(c).at[bsz].set(c_ctx)
    mod = _modulation(c_all, w_mod[lyr], b_mod[lyr])
    mod_l, mod_c = mod[:bsz], mod[bsz]
    n_tiles_ctx = n_ctx // ROW_TILE
    n_tiles = (n_ctx + n_lat) // ROW_TILE
    mod1_l = mod_l[:, None, :2 * D_MODEL]
    mod1_c = jnp.broadcast_to(mod_c[None, None, :2 * D_MODEL], (bsz, 1, 2 * D_MODEL))
    mod1 = jnp.concatenate([jnp.repeat(mod1_c, n_tiles_ctx, axis=1),
                            jnp.repeat(mod1_l, n_tiles - n_tiles_ctx, axis=1)], axis=1)[:, :, None, :]
    mod2 = mod_l[:, None, 2 * D_MODEL:5 * D_MODEL]
    g2 = mod_l[:, None, 5 * D_MODEL:6 * D_MODEL]

    w_main = w_in[lyr][:, :D_MAIN].astype(BF16)
    w_gates = w_in[lyr][:, D_MAIN:]
    p, gates, gates_t = _input_projection(ctx, x, mod1, w_main, w_gates, ml_gate_b[lyr])

    qk = _qk_conv(p, ml_conv_w[lyr], ml_conv_b[lyr], n_ctx)
    o_f, o_b = _token_mixers(p, qk, gates, gates_t, hg_lb_logits[:, lyr:lyr + 2], n_ctx)
    x1, h2t_hi, h2t_lo = _readout_project(o_f, o_b, p, x, mod2, w_out[lyr], hg_norm_g[lyr], ml_norm_g[lyr],
                                          ln1_g[lyr], ln1_b[lyr], n_ctx)
    r2, ci, a_fac, b_fac = _peer_select(h2t_hi, h2t_lo, peer_wq[lyr], peer_keys[lyr])
    u_bf = peer_u[lyr].astype(BF16)
    vt_bf = peer_v[lyr].astype(BF16).T
    return _peer_dense(h2t_hi, r2, ci, a_fac, b_fac, u_bf, vt_bf, x1, g2, ln2_g[lyr], ln2_b[lyr])
```

```python
import functools

import jax
import jax.numpy as jnp
from jax import lax
from jax.experimental import pallas as pl
from jax.experimental.pallas import tpu as pltpu

F32 = jnp.float32
BF16 = jnp.bfloat16

D_MODEL = 1024
CHUNK = 64
GRID_W = 64
N_HEADS = 4
D_HEAD = 128
MIX_W = N_HEADS * D_HEAD
N_GATES = 4 * N_HEADS
D_MAIN = 9 * MIX_W
PEER_HEADS = 8
PEER_NKEYS = 128
PEER_TOPK = 16
PEER_DQ = 256
LN_EPS = 1e-6
DEEPNORM_ALPHA = 2.0 ** 0.25
LANES = 128
VMEM_LIMIT = 56 * 1024 * 1024

COL_HQ, COL_HI, COL_HG, COL_HF0, COL_HF1, COL_MQ, COL_MK, COL_MV, COL_MO = range(9)

ROW_TILE = 256
SEL_TILE = 256
PEER_TT = 512
PEER_ET = 2048

NT_DIMS = (((1,), (1,)), ((), ()))
TN_DIMS = (((0,), (0,)), ((), ()))


def _cparams(sem):
    return pltpu.CompilerParams(dimension_semantics=sem, vmem_limit_bytes=VMEM_LIMIT)


def _split(a, n):
    parts = []
    r = a
    for k in range(n):
        p = r.astype(BF16)
        parts.append(p)
        if k + 1 < n:
            r = r - p.astype(F32)
    return parts


def _mm(a, b, dims=None):
    if dims is None:
        return jnp.dot(a, b, preferred_element_type=F32)
    return lax.dot_general(a, b, dims, preferred_element_type=F32)


def _mm_x3(a, b, dims=None):
    a_hi, a_lo = _split(a, 2)
    b_hi, b_lo = _split(b, 2)
    return _mm(a_hi, b_hi, dims) + (_mm(a_hi, b_lo, dims) + _mm(a_lo, b_hi, dims))


def _mm_ones_left(m_bf16, x):
    x0, x1 = _split(x, 2)
    return _mm(m_bf16, x0) + _mm(m_bf16, x1)


def _mm_ones_right(x, m_bf16):
    x0, x1 = _split(x, 2)
    return _mm(x0, m_bf16) + _mm(x1, m_bf16)


def _sigmoid(x):
    return 1.0 / (1.0 + jnp.exp(-x))


def _silu(x):
    return x * _sigmoid(x)


def _log_sigmoid(x):
    return jnp.minimum(x, 0.0) - jnp.log(1.0 + jnp.exp(-jnp.abs(x)))


def _ln(x):
    mu = jnp.mean(x, axis=-1, keepdims=True)
    xc = x - mu
    var = jnp.mean(xc * xc, axis=-1, keepdims=True)
    return xc * lax.rsqrt(var + LN_EPS)


def _gelu(x):
    return 0.5 * x * (1.0 + lax.erf(x * (2.0 ** -0.5)))


def _mod_kernel(c_ref, w_ref, b_ref, o_ref):
    s = _silu(c_ref[...])
    o_ref[...] = _mm_x3(s, w_ref[...]) + b_ref[...]


def _modulation(c_all, w_mod, b_mod):
    n = c_all.shape[0]
    d_out = w_mod.shape[1]
    blk = 1024
    return pl.pallas_call(
        _mod_kernel,
        grid=(d_out // blk,),
        in_specs=[pl.BlockSpec((n, D_MODEL), lambda j: (0, 0)),
                  pl.BlockSpec((D_MODEL, blk), lambda j: (0, j)),
                  pl.BlockSpec((1, blk), lambda j: (0, j))],
        out_specs=pl.BlockSpec((n, blk), lambda j: (0, j)),
        out_shape=jax.ShapeDtypeStruct((n, d_out), F32),
        compiler_params=_cparams(("arbitrary",)),
        name="mod",
    )(c_all, w_mod, b_mod.reshape(1, d_out))


def _inproj_kernel(n_ctx_tiles, ctx_ref, x_ref, mod_ref, w_ref, gb_ref, u_ref, v_ref,
                   p_ref, g_ref, gt_ref, u_bf_ref, vt_bf_ref):
    u_bf_ref[...] = u_ref[...].astype(BF16)
    vt_bf_ref[...] = v_ref[...].T.astype(BF16)
    x = jnp.where(pl.program_id(1) < n_ctx_tiles, ctx_ref[0], x_ref[0])
    shift = mod_ref[0, 0, :, :D_MODEL]
    scale = mod_ref[0, 0, :, D_MODEL:]
    h = _ln(x) * (1.0 + scale) + shift
    h_hi, h_lo = _split(h, 2)
    res = _mm(h_hi, w_ref[...])
    p_ref[0] = res[:, :D_MAIN]
    g = (res[:, D_MAIN:D_MAIN + LANES]
         + (res[:, D_MAIN + LANES:] + _mm(h_lo, w_ref[:, D_MAIN:D_MAIN + LANES]))) + gb_ref[...]
    g_ref[0] = g
    gt_ref[0] = g.T[:N_GATES, :]


def _input_projection(ctx, x, mod1, w_main, wg, gate_b, peer_u, peer_v):
    bsz, n_ctx, _ = ctx.shape
    n_ctx_tiles = n_ctx // ROW_TILE
    t_all = n_ctx + x.shape[1]
    n_tiles = t_all // ROW_TILE
    n_exp = peer_u.shape[0]
    tab_rows = ROW_TILE
    while n_exp // tab_rows > bsz * n_tiles:
        tab_rows *= 2
    assert n_exp % tab_rows == 0
    tab_blk = lambda b, t: jnp.minimum(b * n_tiles + t, n_exp // tab_rows - 1)
    wg_pad = jnp.zeros((D_MODEL, LANES), F32).at[:, :N_GATES].set(wg)
    wg_hi = wg_pad.astype(BF16)
    wg_lo = (wg_pad - wg_hi.astype(F32)).astype(BF16)
    w_cat = jnp.concatenate([w_main, wg_hi, wg_lo], axis=1)
    gb = jnp.zeros((1, LANES), F32).at[0, :N_GATES].set(gate_b)
    const2 = lambda b, t: (0, 0)
    return pl.pallas_call(
        functools.partial(_inproj_kernel, n_ctx_tiles),
        grid=(bsz, n_tiles),
        in_specs=[pl.BlockSpec((1, ROW_TILE, D_MODEL), lambda b, t: (b, jnp.minimum(t, n_ctx_tiles - 1), 0)),
                  pl.BlockSpec((1, ROW_TILE, D_MODEL), lambda b, t: (b, jnp.maximum(t - n_ctx_tiles, 0), 0)),
                  pl.BlockSpec((1, 1, 1, 2 * D_MODEL), lambda b, t: (b, t, 0, 0)),
                  pl.BlockSpec((D_MODEL, D_MAIN + 2 * LANES), const2),
                  pl.BlockSpec((1, LANES), const2),
                  pl.BlockSpec((tab_rows, D_MODEL), lambda b, t: (tab_blk(b, t), 0)),
                  pl.BlockSpec((tab_rows, D_MODEL), lambda b, t: (tab_blk(b, t), 0))],
        out_specs=[pl.BlockSpec((1, ROW_TILE, D_MAIN), lambda b, t: (b, t, 0)),
                   pl.BlockSpec((1, ROW_TILE, LANES), lambda b, t: (b, t, 0)),
                   pl.BlockSpec((1, N_GATES, ROW_TILE), lambda b, t: (b, 0, t)),
                   pl.BlockSpec((tab_rows, D_MODEL), lambda b, t: (tab_blk(b, t), 0)),
                   pl.BlockSpec((D_MODEL, tab_rows), lambda b, t: (0, tab_blk(b, t)))],
        out_shape=[jax.ShapeDtypeStruct((bsz, t_all, D_MAIN), F32),
                   jax.ShapeDtypeStruct((bsz, t_all, LANES), F32),
                   jax.ShapeDtypeStruct((bsz, N_GATES, t_all), F32),
                   jax.ShapeDtypeStruct((n_exp, D_MODEL), BF16),
                   jax.ShapeDtypeStruct((D_MODEL, n_exp), BF16)],
        compiler_params=_cparams(("arbitrary", "arbitrary")),
        name="inproj",
    )(ctx, x, mod1, w_cat, gb, peer_u, peer_v)


CONV_PAD = 72
CONV_CB = 256


def _conv_kernel(n_ctx, n_lat, p_ref, w_ref, b_ref, o_ref, pad_c, pad_l):
    cb = pl.program_id(1)
    w = w_ref[...]
    bias = b_ref[...]
    scale = jnp.where(cb >= MIX_W // CONV_CB, D_HEAD ** -0.5, 1.0).astype(F32)

    def finish(y):
        return _silu(y + bias) * scale

    pad_c[...] = jnp.zeros_like(pad_c)
    pad_c[pl.ds(CONV_PAD, n_ctx), :] = p_ref[0, pl.ds(0, n_ctx), :]
    y = jnp.zeros((n_ctx, CONV_CB), F32)
    for dc in range(3):
        y = y + pad_c[pl.ds(CONV_PAD + dc - 1, n_ctx), :] * w[3 + dc:4 + dc, :]
    o_ref[0, pl.ds(0, n_ctx), :] = finish(y)

    x = p_ref[0, pl.ds(n_ctx, n_lat), :]
    col = lax.broadcasted_iota(jnp.int32, (n_lat, CONV_CB), 0) & (GRID_W - 1)
    pad_l[...] = jnp.zeros_like(pad_l)
    pad_l[0, pl.ds(CONV_PAD + 1, n_lat), :] = jnp.where(col == GRID_W - 1, 0.0, x)
    pad_l[1, pl.ds(CONV_PAD, n_lat), :] = x
    pad_l[2, pl.ds(CONV_PAD - 1, n_lat), :] = jnp.where(col == 0, 0.0, x)
    y = jnp.zeros((n_lat, CONV_CB), F32)
    for dr in range(3):
        for dc in range(3):
            rows = pl.ds(CONV_PAD + (dr - 1) * GRID_W, n_lat)
            y = y + pad_l[dc, rows, :] * w[3 * dr + dc:3 * dr + dc + 1, :]
    o_ref[0, pl.ds(n_ctx, n_lat), :] = finish(y)


def _qk_conv(p, conv_w, conv_b, n_ctx):
    bsz, t_all, _ = p.shape
    n_lat = t_all - n_ctx
    n_cb = 2 * MIX_W // CONV_CB
    first = COL_MQ * MIX_W // CONV_CB
    w9 = conv_w.reshape(9, 2 * MIX_W)
    return pl.pallas_call(
        functools.partial(_conv_kernel, n_ctx, n_lat),
        grid=(bsz, n_cb),
        in_specs=[pl.BlockSpec((1, t_all, CONV_CB), lambda b, c: (b, 0, first + c)),
                  pl.BlockSpec((9, CONV_CB), lambda b, c: (0, c)),
                  pl.BlockSpec((1, CONV_CB), lambda b, c: (0, c))],
        out_specs=pl.BlockSpec((1, t_all, CONV_CB), lambda b, c: (b, 0, c)),
        out_shape=jax.ShapeDtypeStruct((bsz, t_all, 2 * MIX_W), F32),
        scratch_shapes=[pltpu.VMEM((n_ctx + 2 * CONV_PAD, CONV_CB), F32),
                        pltpu.VMEM((3, n_lat + 2 * CONV_PAD, CONV_CB), F32)],
        compiler_params=_cparams(("arbitrary", "arbitrary")),
        name="qkconv",
    )(p, w9, conv_b.reshape(1, 2 * MIX_W))


MIX_BLOCK = 256
MIX_BATCH = 2


def _block_mask(t, rev):
    r = lax.broadcasted_iota(jnp.int32, (t, t), 0)
    c = lax.broadcasted_iota(jnp.int32, (t, t), 1)
    shift = CHUNK.bit_length() - 1
    same = (r >> shift) == (c >> shift)
    return jnp.logical_and(same, (c >= r) if rev else (c <= r))


def _chunk_rows(x, idx):
    return jnp.concatenate(
        [jnp.broadcast_to(x[j * CHUNK + idx:j * CHUNK + idx + 1], (CHUNK, x.shape[1]))
         for j in range(x.shape[0] // CHUNK)], axis=0)


def _scan_order(n, rev):
    return range(n - 1, -1, -1) if rev else range(n)


def _chunk_expand(x):
    n = x.shape[0] // CHUNK
    chunk = lax.broadcasted_iota(jnp.int32, x.shape, 0) >> (CHUNK.bit_length() - 1)
    return jnp.concatenate([jnp.where(chunk == j, x, jnp.zeros((), x.dtype)) for j in range(n)], axis=1)


def _diag_blocks(y, w):
    n = y.shape[0] // CHUNK
    return jnp.concatenate([y[j * CHUNK:(j + 1) * CHUNK, j * w:(j + 1) * w] for j in range(n)], axis=0)


def _hgrn_dir(d, rev, want_out, q_ref, v_ref, z_ref, lb, st_ref, o_ref):
    vb = v_ref[0].astype(BF16)
    sig = _sigmoid(z_ref[0])
    log_f = jnp.log(lb + (1.0 - lb) * sig)
    kk = (1.0 - lb) * (1.0 - sig)
    t = vb.shape[0]
    n_sub = t // CHUNK
    mask = _block_mask(t, rev)
    tri = jnp.where(mask, 1.0, 0.0).astype(BF16)
    b = _mm_ones_left(tri, log_f)
    last = 0 if rev else CHUNK - 1
    mid = CHUNK // 2 - 1 if rev else CHUNK // 2
    kdec = (kk * jnp.exp(_chunk_rows(b, last) - b)).astype(BF16)
    if want_out:
        q = q_ref[0]
        b_mid = _chunk_rows(b, mid)
        qd = (q * jnp.exp(b - b_mid)).astype(BF16)
        kd = (kk * jnp.exp(b_mid - b)).astype(BF16)
        qe = (q * jnp.exp(b)).astype(BF16)
    for h in range(N_HEADS):
        sl = slice(h * D_HEAD, (h + 1) * D_HEAD)
        upd = _mm(vb[:, sl], _chunk_expand(kdec[:, sl]), TN_DIMS)
        st = st_ref[d, h]
        before = [None] * n_sub
        for j in _scan_order(n_sub, rev):
            before[j] = st
            e_last = jnp.exp(b[j * CHUNK + last:j * CHUNK + last + 1, sl])
            st = st * e_last + upd[:, j * D_HEAD:(j + 1) * D_HEAD]
        st_ref[d, h] = st
        if want_out:
            sc = _mm(qd[:, sl], kd[:, sl], NT_DIMS)
            o = _mm(jnp.where(mask, sc, 0.0).astype(BF16), vb[:, sl])
            stacked = jnp.concatenate([s_.astype(BF16) for s_ in before], axis=0)
            inter = _diag_blocks(_mm(qe[:, sl], stacked, NT_DIMS), D_HEAD)
            o_ref[0, :, sl] = (o + inter).astype(o_ref.dtype)


def _mlstm_dir(d, rev, want_out, qk_ref, v_ref, gc_ref, gr_ref, c_ref, n_ref, m_ref, o_ref):
    qk = qk_ref[0]
    vb = v_ref[0].astype(BF16)
    gc = gc_ref[0]
    gr = gr_ref[0]
    t = gc.shape[0]
    n_sub = t // CHUNK
    mask = _block_mask(t, rev)
    tri = jnp.where(mask, 1.0, 0.0).astype(BF16)
    a_cols = _mm_ones_left(tri, _log_sigmoid(gc))
    if want_out:
        tri_t = jnp.where(_block_mask(t, not rev), 1.0, 0.0).astype(BF16)
        a_rows = _mm_ones_right(_log_sigmoid(gr), tri_t)
    last = 0 if rev else CHUNK - 1
    order = _scan_order(n_sub, rev)

    log_i = pltpu.roll(gc, 2 * N_HEADS, axis=1)
    g = _chunk_rows(a_cols, last) - a_cols + log_i
    m_loc = [jnp.max(g[j * CHUNK:(j + 1) * CHUNK], axis=0, keepdims=True) for j in range(n_sub)]
    w_in = jnp.exp(g - jnp.concatenate([jnp.broadcast_to(r, (CHUNK, LANES)) for r in m_loc], axis=0))
    m = m_ref[d]
    m_before, s_old, s_new = [None] * n_sub, [None] * n_sub, [None] * n_sub
    for j in order:
        a_end = a_cols[j * CHUNK + last:j * CHUNK + last + 1]
        m_before[j] = m
        m_new = jnp.maximum(a_end + m, m_loc[j])
        s_old[j] = jnp.exp(a_end + m - m_new)
        s_new[j] = jnp.exp(m_loc[j] - m_new)
        m = m_new
    m_ref[d] = m
    m_prev_all = jnp.concatenate([jnp.broadcast_to(r, (CHUNK, LANES)) for r in m_before], axis=0)

    for h in range(N_HEADS):
        ic = d * N_HEADS + h
        fc = 2 * N_HEADS + d * N_HEADS + h
        q_h = qk[:, h * D_HEAD:(h + 1) * D_HEAD]
        k_h = qk[:, MIX_W + h * D_HEAD:MIX_W + (h + 1) * D_HEAD]
        v_h = vb[:, h * D_HEAD:(h + 1) * D_HEAD]
        qb = q_h.astype(BF16)
        wk = w_in[:, fc:fc + 1] * k_h
        wkb = wk.astype(BF16)
        upd = _mm(wkb, _chunk_expand(v_h), TN_DIMS)
        c_st = c_ref[d, h]
        n_st = n_ref[d, h]
        c_before, qn = [None] * n_sub, [None] * n_sub
        for j in order:
            rows = slice(j * CHUNK, (j + 1) * CHUNK)
            c_before[j] = c_st
            if want_out:
                qn[j] = jnp.sum(q_h[rows] * n_st, axis=1, keepdims=True)
            so = s_old[j][:, fc:fc + 1]
            sn = s_new[j][:, fc:fc + 1]
            c_st = so * c_st + sn * upd[:, j * D_HEAD:(j + 1) * D_HEAD]
            n_st = so * n_st + sn * jnp.sum(wk[rows], axis=0, keepdims=True)
        c_ref[d, h] = c_st
        n_ref[d, h] = n_st
        if not want_out:
            continue
        side = jnp.concatenate([c_.astype(BF16) for c_ in c_before], axis=1)
        inter = _diag_blocks(_mm(qb, side), D_HEAD)

        r_row = gr[ic:ic + 1, :] - a_rows[fc:fc + 1, :]
        m_prev = m_prev_all[:, fc:fc + 1]
        r_max = jnp.max(jnp.where(mask, r_row, -jnp.inf), axis=1, keepdims=True)
        mm = jnp.maximum(r_max, m_prev)
        w_intra = _mm(qb, k_h.astype(BF16), NT_DIMS) * jnp.exp(jnp.where(mask, r_row - mm, -jnp.inf))
        w_inter = jnp.exp(m_prev - mm)
        num = _mm(w_intra.astype(BF16), v_h) + w_inter * inter
        den = jnp.sum(w_intra, axis=1, keepdims=True) + w_inter * jnp.concatenate(qn, axis=0)
        floor = jnp.exp(-(a_cols[:, fc:fc + 1] + mm))
        o_ref[0, :, MIX_W + h * D_HEAD:MIX_W + (h + 1) * D_HEAD] = (
            num / jnp.maximum(jnp.abs(den), floor)).astype(o_ref.dtype)


def _mix_kernel(n_cb, lbl_ref,
                hq_f, hi_f, hf_f, mv_f, qk_f, gc_f, gr_f,
                hq_b, hi_b, hf_b, mv_b, qk_b, gc_b, gr_b,
                of_ref, ob_ref, st_ref, c_ref, n_ref, m_ref):
    s = pl.program_id(1)

    @pl.when(s == 0)
    def _():
        st_ref[...] = jnp.zeros_like(st_ref)
        c_ref[...] = jnp.zeros_like(c_ref)
        n_ref[...] = jnp.zeros_like(n_ref)
        m_ref[...] = jnp.zeros_like(m_ref)

    def step(want_out):
        for d, rev, dir_refs in (
                (0, False, (hq_f, hi_f, hf_f, mv_f, qk_f, gc_f, gr_f, of_ref)),
                (1, True, (hq_b, hi_b, hf_b, mv_b, qk_b, gc_b, gr_b, ob_ref))):
            logits = lbl_ref[d]
            mx = jnp.max(logits, axis=0, keepdims=True)
            ex = jnp.exp(logits - mx)
            lb = ex[0:1, :] / jnp.sum(ex, axis=0, keepdims=True)
            for bb in range(MIX_BATCH):
                hq, hi, hf, mv, qk, gc, gr, o_ref = (r.at[pl.ds(bb, 1)] for r in dir_refs)
                _hgrn_dir(d, rev, want_out, hq, hi, hf, lb, st_ref.at[bb], o_ref)
                _mlstm_dir(d, rev, want_out, qk, mv, gc, gr, c_ref.at[bb], n_ref.at[bb], m_ref.at[bb], o_ref)

    @pl.when(s < n_cb)
    def _():
        step(False)

    @pl.when(s >= n_cb)
    def _():
        step(True)


def _token_mixers(p, qk, gates, gates_t, lb_logits, n_ctx):
    bsz, t_all, _ = p.shape
    n_lat = t_all - n_ctx
    n_cb = n_ctx // MIX_BLOCK
    n_blk = t_all // MIX_BLOCK

    def f_idx(s):
        return s

    def b_idx(s):
        return jnp.where(s < n_cb, n_cb - 1 - s, n_blk - 1 + n_cb - s)

    nb = MIX_BATCH
    assert bsz % nb == 0

    def pspec(col, idx):
        return pl.BlockSpec((nb, MIX_BLOCK, MIX_W), lambda b, s: (b, idx(s), col))

    def dir_specs(idx, col_f):
        return [pspec(COL_HQ, idx), pspec(COL_HI, idx), pspec(col_f, idx), pspec(COL_MV, idx),
                pl.BlockSpec((nb, MIX_BLOCK, 2 * MIX_W), lambda b, s: (b, idx(s), 0)),
                pl.BlockSpec((nb, MIX_BLOCK, LANES), lambda b, s: (b, idx(s), 0)),
                pl.BlockSpec((nb, N_GATES, MIX_BLOCK), lambda b, s: (b, 0, idx(s)))]

    out_f = pl.BlockSpec((nb, MIX_BLOCK, 2 * MIX_W), lambda b, s: (b, jnp.maximum(s - n_cb, 0), 0))
    out_b = pl.BlockSpec((nb, MIX_BLOCK, 2 * MIX_W),
                         lambda b, s: (b, jnp.minimum(n_blk - 1 - s, n_blk - 1 - n_cb), 0))
    args_dir = [p, p, p, p, qk, gates, gates_t]
    return pl.pallas_call(
        functools.partial(_mix_kernel, n_cb),
        grid=(bsz // nb, n_blk),
        in_specs=[pl.BlockSpec((2, 2, MIX_W), lambda b, s: (0, 0, 0))]
        + dir_specs(f_idx, COL_HF0) + dir_specs(b_idx, COL_HF1),
        out_specs=[out_f, out_b],
        out_shape=[jax.ShapeDtypeStruct((bsz, n_lat, 2 * MIX_W), BF16)] * 2,
        scratch_shapes=[pltpu.VMEM((nb, 2, N_HEADS, D_HEAD, D_HEAD), F32),
                        pltpu.VMEM((nb, 2, N_HEADS, D_HEAD, D_HEAD), F32),
                        pltpu.VMEM((nb, 2, N_HEADS, 1, D_HEAD), F32),
                        pltpu.VMEM((nb, 2, 1, LANES), F32)],
        compiler_params=_cparams(("arbitrary", "arbitrary")),
        name="mix",
    )(lb_logits, *args_dir, *args_dir)


def _out_kernel(of_ref, ob_ref, hg_ref, mo_ref, x_ref, mod_ref, w_ref, hgn_ref, mln_ref, lng_ref, lnb_ref,
                x1_ref, h2t_hi_ref, h2t_lo_ref):
    o = of_ref[0].astype(F32) + ob_ref[0].astype(F32)
    hg = hg_ref[0]
    mo = mo_ref[0]
    parts = []
    for h in range(N_HEADS):
        sl = slice(h * D_HEAD, (h + 1) * D_HEAD)
        oh = o[:, sl]
        y = oh * lax.rsqrt(jnp.mean(oh * oh, axis=-1, keepdims=True) + LN_EPS) * hgn_ref[:, sl]
        parts.append(y * _silu(hg[:, sl]))
    for h in range(N_HEADS):
        sl = slice(h * D_HEAD, (h + 1) * D_HEAD)
        oh = o[:, MIX_W + h * D_HEAD:MIX_W + (h + 1) * D_HEAD]
        parts.append(_ln(oh) * mln_ref[:, sl] * _sigmoid(mo[:, sl]))
    mix = jnp.concatenate(parts, axis=-1).astype(BF16)
    y = _mm(mix, w_ref[...])
    g1 = mod_ref[0, :, 0:D_MODEL]
    sh2 = mod_ref[0, :, D_MODEL:2 * D_MODEL]
    sc2 = mod_ref[0, :, 2 * D_MODEL:3 * D_MODEL]
    x1 = _ln(DEEPNORM_ALPHA * x_ref[0] + g1 * y) * lng_ref[...] + lnb_ref[...]
    x1_ref[0] = x1
    h_hi, h_lo = _split((_ln(x1) * (1.0 + sc2) + sh2).T, 2)
    h2t_hi_ref[...] = h_hi
    h2t_lo_ref[...] = h_lo


def _readout_project(o_f, o_b, p, x, mod2, w_out, hg_norm_g, ml_norm_g, ln1_g, ln1_b, n_ctx):
    bsz, n_lat, _ = x.shape
    n_tiles = n_lat // ROW_TILE
    off = n_ctx // ROW_TILE
    row = lambda a: a.reshape(1, -1)
    const2 = lambda b, t: (0, 0)
    return pl.pallas_call(
        _out_kernel,
        grid=(bsz, n_tiles),
        in_specs=[pl.BlockSpec((1, ROW_TILE, 2 * MIX_W), lambda b, t: (b, t, 0)),
                  pl.BlockSpec((1, ROW_TILE, 2 * MIX_W), lambda b, t: (b, t, 0)),
                  pl.BlockSpec((1, ROW_TILE, MIX_W), lambda b, t: (b, t + off, COL_HG)),
                  pl.BlockSpec((1, ROW_TILE, MIX_W), lambda b, t: (b, t + off, COL_MO)),
                  pl.BlockSpec((1, ROW_TILE, D_MODEL), lambda b, t: (b, t, 0)),
                  pl.BlockSpec((1, 1, 3 * D_MODEL), lambda b, t: (b, 0, 0)),
                  pl.BlockSpec((2 * MIX_W, D_MODEL), const2),
                  pl.BlockSpec((1, MIX_W), const2),
                  pl.BlockSpec((1, MIX_W), const2),
                  pl.BlockSpec((1, D_MODEL), const2),
                  pl.BlockSpec((1, D_MODEL), const2)],
        out_specs=[pl.BlockSpec((1, ROW_TILE, D_MODEL), lambda b, t: (b, t, 0)),
                   pl.BlockSpec((D_MODEL, ROW_TILE), lambda b, t: (0, b * n_tiles + t)),
                   pl.BlockSpec((D_MODEL, ROW_TILE), lambda b, t: (0, b * n_tiles + t))],
        out_shape=[jax.ShapeDtypeStruct((bsz, n_lat, D_MODEL), F32),
                   jax.ShapeDtypeStruct((D_MODEL, bsz * n_lat), BF16),
                   jax.ShapeDtypeStruct((D_MODEL, bsz * n_lat), BF16)],
        compiler_params=_cparams(("arbitrary", "arbitrary")),
        name="readout",
    )(o_f, o_b, p, p, x, mod2, w_out.astype(BF16), row(hg_norm_g), row(ml_norm_g), row(ln1_g), row(ln1_b))


def _extract_exact(s):
    ridx = lax.broadcasted_iota(jnp.int32, s.shape, 0).astype(F32)
    rank = jnp.full(s.shape, float(PEER_TOPK), F32)
    vals = []
    for it in range(PEER_TOPK):
        m = jnp.max(s, axis=0, keepdims=True)
        first = jnp.min(jnp.where(s == m, ridx, float(s.shape[0])), axis=0, keepdims=True)
        hit = ridx == first
        rank = jnp.where(hit, float(it), rank)
        s = jnp.where(hit, -jnp.inf, s)
        vals.append(m)
    return rank, vals


def _extract_fast(s):
    rank = jnp.full(s.shape, float(PEER_TOPK), F32)
    vals = []
    for it in range(PEER_TOPK):
        m = jnp.max(s, axis=0, keepdims=True)
        hit = s == m
        rank = jnp.where(hit, float(it), rank)
        s = jnp.where(hit, -jnp.inf, s)
        vals.append(m)
    return rank, vals


def _count_true(mask):
    return jnp.sum(jnp.where(mask, 1.0, 0.0), axis=0, keepdims=True)


def _cand_grid(v1, v2):
    v1a = jnp.concatenate(v1, axis=0)
    v2a = jnp.concatenate(v2, axis=0)
    row = lax.broadcasted_iota(jnp.int32, (8, v1a.shape[1]), 0)
    blocks = [v1a[0:1] + v2a]
    for a in range(1, 8):
        nb = PEER_TOPK // (a + 1)
        blk = v1a[a:a + 1] + v2a[0:8]
        blocks.append(blk if nb >= 8 else jnp.where(row < nb, blk, -jnp.inf))
    blocks.append(v1a[8:16] + v2a[0:1])
    return jnp.concatenate(blocks, axis=0)


def _cand_counts(chosen):
    cnt = [_count_true(chosen[0:PEER_TOPK])]
    for a in range(1, 8):
        cnt.append(_count_true(chosen[PEER_TOPK + 8 * (a - 1):PEER_TOPK + 8 * a]))
    base = PEER_TOPK + 8 * 7
    for r in range(8):
        cnt.append(jnp.where(chosen[base + r:base + r + 1], 1.0, 0.0))
    return cnt


def _sel_emit(hd, s1, s2, is_rank1, rank2, v1, v2, cand, chosen, r2_ref, ci_ref, a_ref, b_ref):
    top = v1[0] + v2[0]
    z = jnp.sum(jnp.where(chosen, jnp.exp(cand - top), 0.0), axis=0, keepdims=True)
    ci = jnp.zeros_like(s1)
    for a, cnt_a in enumerate(_cand_counts(chosen)):
        ci = jnp.where(is_rank1(a), cnt_a, ci)
    r2_ref[hd] = rank2.astype(BF16)
    ci_ref[hd] = ci
    a_ref[hd] = jnp.exp(s1 - v1[0])
    b_ref[hd] = (jnp.exp(s2 - v2[0]) / z).astype(BF16)


def _sel_kernel(h_hi_ref, h_lo_ref, wq_hi_ref, wq_lo_ref, k_hi_ref, k_lo_ref, r2_ref, ci_ref, a_ref, b_ref):
    h_hi = h_hi_ref[...]
    h_lo = h_lo_ref[...]
    qt = (_mm(wq_hi_ref[...], h_hi) + (_mm(wq_lo_ref[...], h_hi) + _mm(wq_hi_ref[...], h_lo)))
    half = PEER_DQ // 2
    for hd in range(PEER_HEADS):
        s = []
        for p_ in range(2):
            qh = qt[hd * PEER_DQ + p_ * half: hd * PEER_DQ + (p_ + 1) * half, :]
            q_hi, q_lo = _split(qh, 2)
            k_hi = k_hi_ref[hd, p_]
            k_lo = k_lo_ref[hd, p_]
            s.append(_mm(k_hi, q_hi) + (_mm(k_lo, q_hi) + _mm(k_hi, q_lo)))
        s1, s2 = s

        _, v1 = _extract_fast(s1)
        rank2, v2 = _extract_fast(s2)
        cand = _cand_grid(v1, v2)
        _, cv = _extract_fast(cand)
        chosen = cand >= cv[PEER_TOPK - 1]
        k = float(PEER_TOPK)
        tied = ((_count_true(s1 >= v1[PEER_TOPK - 1]) != k) | (_count_true(rank2 < k) != k)
                | (_count_true(chosen) != k))
        any_tied = jnp.max(jnp.where(tied, 1.0, 0.0)) > 0.0
        _sel_emit(hd, s1, s2, lambda a: s1 == v1[a], rank2, v1, v2, cand, chosen, r2_ref, ci_ref, a_ref, b_ref)

        @pl.when(any_tied)
        def _():
            rank1, v1 = _extract_exact(s1)
            rank2, v2 = _extract_exact(s2)
            cand = _cand_grid(v1, v2)
            crank, _ = _extract_exact(cand)
            _sel_emit(hd, s1, s2, lambda a: rank1 == float(a), rank2, v1, v2, cand, crank < k,
                      r2_ref, ci_ref, a_ref, b_ref)


def _peer_select(h2t_hi, h2t_lo, peer_wq, peer_keys):
    n_tok = h2t_hi.shape[1]
    wqt = peer_wq.T
    wq_hi = wqt.astype(BF16)
    wq_lo = (wqt - wq_hi.astype(F32)).astype(BF16)
    k_hi = peer_keys.astype(BF16)
    k_lo = (peer_keys - k_hi.astype(F32)).astype(BF16)
    dq_all = PEER_HEADS * PEER_DQ
    out_spec = pl.BlockSpec((PEER_HEADS, PEER_NKEYS, SEL_TILE), lambda t: (0, 0, t))
    shape = (PEER_HEADS, PEER_NKEYS, n_tok)
    return pl.pallas_call(
        _sel_kernel,
        grid=(n_tok // SEL_TILE,),
        in_specs=[pl.BlockSpec((D_MODEL, SEL_TILE), lambda t: (0, t)),
                  pl.BlockSpec((D_MODEL, SEL_TILE), lambda t: (0, t)),
                  pl.BlockSpec((dq_all, D_MODEL), lambda t: (0, 0)),
                  pl.BlockSpec((dq_all, D_MODEL), lambda t: (0, 0)),
                  pl.BlockSpec((PEER_HEADS, 2, PEER_NKEYS, PEER_DQ // 2), lambda t: (0, 0, 0, 0)),
                  pl.BlockSpec((PEER_HEADS, 2, PEER_NKEYS, PEER_DQ // 2), lambda t: (0, 0, 0, 0))],
        out_specs=[out_spec] * 4,
        out_shape=[jax.ShapeDtypeStruct(shape, BF16), jax.ShapeDtypeStruct(shape, F32),
                   jax.ShapeDtypeStruct(shape, F32), jax.ShapeDtypeStruct(shape, BF16)],
        compiler_params=_cparams(("arbitrary",)),
        name="peersel",
    )(h2t_hi, h2t_lo, wq_hi, wq_lo, k_hi, k_lo)


def _rows_bf16(row):
    tile = jnp.broadcast_to(row, (16, row.shape[1])).astype(BF16)
    return jnp.concatenate([tile] * (PEER_NKEYS // 16), axis=0)


def _peer_kernel(n_e, h_ref, r2_ref, b_ref, ci_ref, a_ref, u_ref, vt_ref, x1_ref, g2_ref, lng_ref, lnb_ref,
                 o_ref, acc_ref, pre_ref, wa_ref):
    e = pl.program_id(1)

    @pl.when(e == 0)
    def _():
        acc_ref[...] = jnp.zeros_like(acc_ref)

    pre_ref[...] = _mm(u_ref[...], h_ref[...])
    zero = jnp.zeros((), BF16)
    for ii in range(PEER_ET // PEER_NKEYS):
        w = None
        for hd in range(PEER_HEADS):
            cnt = _rows_bf16(ci_ref[hd, ii:ii + 1, :])
            fac = _rows_bf16(a_ref[hd, ii:ii + 1, :])
            term = jnp.where(r2_ref[hd] < cnt, b_ref[hd] * fac, zero)
            w = term if w is None else w + term
        rows = pl.ds(ii * PEER_NKEYS, PEER_NKEYS)
        wa_ref[rows, :] = w * _gelu(pre_ref[rows, :].astype(BF16))
    acc_ref[...] += _mm(vt_ref[...], wa_ref[...])

    @pl.when(e == n_e - 1)
    def _():
        y = acc_ref[...].T
        o_ref[0] = _ln(DEEPNORM_ALPHA * x1_ref[0] + g2_ref[0] * y) * lng_ref[...] + lnb_ref[...]


def _peer_dense(h2t_bf, r2, ci, a_fac, b_fac, u_bf, vt_bf, x1, g2, ln2_g, ln2_b):
    bsz, n_lat, _ = x1.shape
    n_tok = bsz * n_lat
    n_t = n_tok // PEER_TT
    per_b = n_lat // PEER_TT
    n_e = u_bf.shape[0] // PEER_ET
    per = PEER_ET // PEER_NKEYS
    assert per % 8 == 0, "whole sublane tiles of half-1 keys per expert step"
    tok_spec = pl.BlockSpec((PEER_HEADS, PEER_NKEYS, PEER_TT), lambda t, e: (0, 0, t))
    key_spec = pl.BlockSpec((PEER_HEADS, per, PEER_TT), lambda t, e: (0, e, t))
    const2 = lambda t, e: (0, 0)
    return pl.pallas_call(
        functools.partial(_peer_kernel, n_e),
        grid=(n_t, n_e),
        in_specs=[pl.BlockSpec((D_MODEL, PEER_TT), lambda t, e: (0, t)),
                  tok_spec, tok_spec, key_spec, key_spec,
                  pl.BlockSpec((PEER_ET, D_MODEL), lambda t, e: (e, 0)),
                  pl.BlockSpec((D_MODEL, PEER_ET), lambda t, e: (0, e)),
                  pl.BlockSpec((1, PEER_TT, D_MODEL), lambda t, e: (t // per_b, t % per_b, 0)),
                  pl.BlockSpec((1, 1, D_MODEL), lambda t, e: (t // per_b, 0, 0)),
                  pl.BlockSpec((1, D_MODEL), const2),
                  pl.BlockSpec((1, D_MODEL), const2)],
        out_specs=pl.BlockSpec((1, PEER_TT, D_MODEL), lambda t, e: (t // per_b, t % per_b, 0)),
        out_shape=jax.ShapeDtypeStruct((bsz, n_lat, D_MODEL), F32),
        scratch_shapes=[pltpu.VMEM((D_MODEL, PEER_TT), F32),
                        pltpu.VMEM((PEER_ET, PEER_TT), F32),
                        pltpu.VMEM((PEER_ET, PEER_TT), BF16)],
        compiler_params=_cparams(("arbitrary", "arbitrary")),
        name="peer",
    )(h2t_bf, r2, b_fac, ci, a_fac, u_bf, vt_bf, x1, g2, ln2_g.reshape(1, -1), ln2_b.reshape(1, -1))


def kernel(x, c, ctx, c_ctx, w_mod, b_mod, w_in, hg_lb_logits, hg_norm_g, ml_conv_w, ml_conv_b, ml_gate_b,
           ml_norm_g, w_out, ln1_g, ln1_b, peer_wq, peer_keys, peer_u, peer_v, ln2_g, ln2_b):
    bsz, n_lat, _ = x.shape
    n_ctx = ctx.shape[1]
    assert n_ctx % ROW_TILE == 0 and n_lat % ROW_TILE == 0 and ROW_TILE % CHUNK == 0
    assert n_ctx % MIX_BLOCK == 0 and n_lat % MIX_BLOCK == 0 and MIX_BLOCK % CHUNK == 0
    assert w_mod.shape[0] == 1, "single-layer kernel"
    lyr = 0

    n_rows = -(-(bsz + 1) // 8) * 8
    c_all = jnp.zeros((n_rows, D_MODEL), F32).at[:bsz].set(c).at[bsz].set(c_ctx)
    mod = _modulation(c_all, w_mod[lyr], b_mod[lyr])
    mod_l, mod_c = mod[:bsz], mod[bsz]
    n_tiles_ctx = n_ctx // ROW_TILE
    n_tiles = (n_ctx + n_lat) // ROW_TILE
    mod1_l = mod_l[:, None, :2 * D_MODEL]
    mod1_c = jnp.broadcast_to(mod_c[None, None, :2 * D_MODEL], (bsz, 1, 2 * D_MODEL))
    mod1 = jnp.concatenate([jnp.repeat(mod1_c, n_tiles_ctx, axis=1),
                            jnp.repeat(mod1_l, n_tiles - n_tiles_ctx, axis=1)], axis=1)[:, :, None, :]
    mod2 = mod_l[:, None, 2 * D_MODEL:5 * D_MODEL]
    g2 = mod_l[:, None, 5 * D_MODEL:6 * D_MODEL]

    w_main = w_in[lyr][:, :D_MAIN].astype(BF16)
    w_gates = w_in[lyr][:, D_MAIN:]
    p, gates, gates_t, u_bf, vt_bf = _input_projection(ctx, x, mod1, w_main, w_gates, ml_gate_b[lyr],
                                                        peer_u[lyr], peer_v[lyr])

    qk = _qk_conv(p, ml_conv_w[lyr], ml_conv_b[lyr], n_ctx)
    o_f, o_b = _token_mixers(p, qk, gates, gates_t, hg_lb_logits[:, lyr:lyr + 2], n_ctx)
    x1, h2t_hi, h2t_lo = _readout_project(o_f, o_b, p, x, mod2, w_out[lyr], hg_norm_g[lyr], ml_norm_g[lyr],
                                          ln1_g[lyr], ln1_b[lyr], n_ctx)
    r2, ci, a_fac, b_fac = _peer_select(h2t_hi, h2t_lo, peer_wq[lyr], peer_keys[lyr])
    return _peer_dense(h2t_hi, r2, ci, a_fac, b_fac, u_bf, vt_bf, x1, g2, ln2_g[lyr], ln2_b[lyr])
```
